```python
import jax, jax.numpy as jnp
from jax import lax
import numpy as np

D_MODEL = 1024
BATCH = 32
SEQ = 2048
DEPTH = 1

HEAD_DIM = 64
MOBA_HEADS = 8
MOBA_BLOCK = 256
MOBA_TOPK = 3
DSA_HEADS = 8
DSA_MAX_TOPK = 256
IDX_HEADS = 8
IDX_DIM = 64
N_GROUPS = 4
EXPERTS_PER_GROUP = 8
N_EXPERTS = N_GROUPS * EXPERTS_PER_GROUP
EXPERT_TOPK = 2
EXPERT_FF = 512
DISPATCH_BLOCK = 256
Q_CHUNK = 128
ROPE_THETA = 10000.0
RMS_EPS = 1e-6
NEG = -1e30

MOBA_WIDTH = MOBA_HEADS * HEAD_DIM
DSA_WIDTH = DSA_HEADS * HEAD_DIM
IDX_SCALE = (IDX_HEADS * IDX_DIM) ** -0.5
IN_SIZES = (MOBA_WIDTH,) * 3 + (DSA_WIDTH,) * 3 + (IDX_HEADS * IDX_DIM, IDX_DIM, IDX_HEADS, D_MODEL, D_MODEL)
IN_COLS = sum(IN_SIZES)
SPLIT_POINTS = tuple(int(v) for v in np.cumsum(IN_SIZES)[:-1])

kernel_name = "hybrid_moba_dsa_hier_moe_block"


def rms_norm(x, g):
    xf = x.astype(jnp.float32)
    y = xf * lax.rsqrt(jnp.mean(xf * xf, axis=-1, keepdims=True) + RMS_EPS)
    return (y * g.astype(jnp.float32)).astype(x.dtype)


def rope(x):
    s, d = x.shape[1], x.shape[-1]
    inv = jnp.power(ROPE_THETA, -jnp.arange(0, d, 2, dtype=jnp.float32) / d)
    ang = jnp.arange(s, dtype=jnp.float32)[:, None] * inv[None, :]
    ang = ang.reshape((s,) + (1,) * (x.ndim - 3) + (d // 2,))
    cos, sin = jnp.cos(ang), jnp.sin(ang)
    x1, x2 = jnp.split(x.astype(jnp.float32), 2, axis=-1)
    return jnp.concatenate([x1 * cos - x2 * sin, x1 * sin + x2 * cos], axis=-1).astype(x.dtype)


def masked_softmax(scores, mask):
    return jax.nn.softmax(jnp.where(mask, scores.astype(jnp.float32), NEG), axis=-1)


def moba_attend(q, k, v):
    h, s, d = q.shape
    nb = -(-s // MOBA_BLOCK)
    pad = nb * MOBA_BLOCK - s
    kb = jnp.pad(k, ((0, 0), (0, pad), (0, 0))).reshape(h, nb, MOBA_BLOCK, d)
    vb = jnp.pad(v, ((0, 0), (0, pad), (0, 0))).reshape(h, nb, MOBA_BLOCK, d)
    n_sel = max(1, min(MOBA_TOPK, nb - 1))
    own = jnp.arange(s) // MOBA_BLOCK
    k_mean = jnp.mean(kb.astype(jnp.float32), axis=2)
    gate = jnp.einsum('hsd,hnd->hsn', q.astype(jnp.float32), k_mean)
    past = jnp.arange(nb)[None, :] < own[:, None]
    gate = jnp.where(past[None], gate, NEG)
    _, sel = lax.top_k(gate, n_sel)
    sel_ok = sel < own[None, :, None]
    n_chunks = s // Q_CHUNK
    qc = q.reshape(h, n_chunks, Q_CHUNK, d).transpose(1, 0, 2, 3)
    selc = sel.reshape(h, n_chunks, Q_CHUNK, n_sel).transpose(1, 0, 2, 3)
    okc = sel_ok.reshape(h, n_chunks, Q_CHUNK, n_sel).transpose(1, 0, 2, 3)
    scale = d ** -0.5
    head_ix = jnp.arange(h)[:, None, None]

    def one_chunk(args):
        qi, si, oki, c = args
        q_pos = c * Q_CHUNK + jnp.arange(Q_CHUNK)
        blk = (c * Q_CHUNK) // MOBA_BLOCK
        k_sel = kb[head_ix, si]
        v_sel = vb[head_ix, si]
        k_own = lax.dynamic_index_in_dim(kb, blk, axis=1, keepdims=False)
        v_own = lax.dynamic_index_in_dim(vb, blk, axis=1, keepdims=False)
        s_sel = jnp.einsum('hqd,hqnkd->hqnk', qi, k_sel).reshape(h, Q_CHUNK, n_sel * MOBA_BLOCK) * scale
        s_own = jnp.einsum('hqd,hkd->hqk', qi, k_own) * scale
        m_sel = jnp.broadcast_to(oki[..., None], oki.shape + (MOBA_BLOCK,)).reshape(h, Q_CHUNK, n_sel * MOBA_BLOCK)
        k_pos = blk * MOBA_BLOCK + jnp.arange(MOBA_BLOCK)
        m_own = jnp.broadcast_to((k_pos[None, :] <= q_pos[:, None])[None], (h, Q_CHUNK, MOBA_BLOCK))
        p = masked_softmax(jnp.concatenate([s_sel, s_own], axis=-1),
                           jnp.concatenate([m_sel, m_own], axis=-1)).astype(v.dtype)
        p_sel = p[..., :n_sel * MOBA_BLOCK].reshape(h, Q_CHUNK, n_sel, MOBA_BLOCK)
        p_own = p[..., n_sel * MOBA_BLOCK:]
        return (jnp.einsum('hqnk,hqnkd->hqd', p_sel, v_sel)
                + jnp.einsum('hqk,hkd->hqd', p_own, v_own))

    out = lax.map(one_chunk, (qc, selc, okc, jnp.arange(n_chunks)))
    return out.transpose(1, 0, 2, 3).reshape(h, s, d)


def dsa_attend(q, k, v, q_idx, k_idx, w_idx):
    h, s, d = q.shape
    top = min(DSA_MAX_TOPK, s // 4)
    n_chunks = s // Q_CHUNK
    qc = q.reshape(h, n_chunks, Q_CHUNK, d).transpose(1, 0, 2, 3)
    qic = q_idx.reshape(n_chunks, Q_CHUNK, IDX_HEADS, IDX_DIM)
    wic = w_idx.reshape(n_chunks, Q_CHUNK, IDX_HEADS)
    k_pos = jnp.arange(s)
    kf = k_idx.astype(jnp.float32)
    scale = d ** -0.5

    def one_chunk(args):
        qi, qii, wi, c = args
        q_pos = c * Q_CHUNK + jnp.arange(Q_CHUNK)
        logits = jnp.einsum('qhd,sd->qhs', qii.astype(jnp.float32), kf)
        score = jnp.einsum('qh,qhs->qs', wi.astype(jnp.float32), jax.nn.relu(logits))
        score = jnp.where(k_pos[None, :] <= q_pos[:, None], score, NEG)
        _, idx = lax.top_k(score, top)
        ok = idx <= q_pos[:, None]
        k_sel = k[:, idx]
        v_sel = v[:, idx]
        sc = jnp.einsum('hqd,hqkd->hqk', qi, k_sel) * scale
        p = masked_softmax(sc, ok[None]).astype(v.dtype)
        return jnp.einsum('hqk,hqkd->hqd', p, v_sel)

    out = lax.map(one_chunk, (qc, qic, wic, jnp.arange(n_chunks)))
    return out.transpose(1, 0, 2, 3).reshape(h, s, d)


def hier_moe(h, w_group, b_group, w_expert, b_expert, w1, w3, w2):
    b, s, d = h.shape
    t = b * s
    xt = h.reshape(t, d)
    g_logit = (xt @ w_group + b_group).astype(jnp.float32)
    g_prob = jax.nn.softmax(g_logit, axis=-1)
    g_sel = jnp.argmax(g_logit, axis=-1)
    g_w = jnp.take_along_axis(g_prob, g_sel[:, None], axis=1)
    e_logit = (xt @ w_expert + b_expert).astype(jnp.float32).reshape(t, N_GROUPS, EXPERTS_PER_GROUP)
    e_logit = jnp.take_along_axis(e_logit, g_sel[:, None, None], axis=1)[:, 0]
    top_p, top_i = lax.top_k(jax.nn.softmax(e_logit, axis=-1), EXPERT_TOPK)
    gate = g_w * top_p / jnp.sum(top_p, axis=-1, keepdims=True)
    expert = g_sel[:, None] * EXPERTS_PER_GROUP + top_i
    n_asg = t * EXPERT_TOPK
    e_flat = expert.reshape(-1)
    order = jnp.argsort(e_flat)
    e_sorted = e_flat[order]
    tok_sorted = order // EXPERT_TOPK
    gate_sorted = gate.reshape(-1)[order].astype(h.dtype)
    counts = jnp.bincount(e_flat, length=N_EXPERTS)
    padded = (counts + DISPATCH_BLOCK - 1) // DISPATCH_BLOCK * DISPATCH_BLOCK
    ends = jnp.cumsum(padded)
    starts_pad = ends - padded
    starts = jnp.cumsum(counts) - counts
    dest = starts_pad[e_sorted] + jnp.arange(n_asg) - starts[e_sorted]
    n_blocks = -(-n_asg // DISPATCH_BLOCK) + N_EXPERTS
    xs = jnp.zeros((n_blocks * DISPATCH_BLOCK, d), h.dtype).at[dest].set(xt[tok_sorted])
    block_expert = jnp.minimum(
        jnp.searchsorted(ends, jnp.arange(n_blocks) * DISPATCH_BLOCK, side='right'), N_EXPERTS - 1)

    def expert_block(args):
        xb, e = args
        hid = jax.nn.silu(xb @ w1[e]) * (xb @ w3[e])
        return hid @ w2[e]

    ys = lax.map(expert_block, (xs.reshape(n_blocks, DISPATCH_BLOCK, d), block_expert)).reshape(-1, d)
    out = jax.ops.segment_sum(ys[dest] * gate_sorted[:, None], tok_sorted, num_segments=t)
    return out.reshape(b, s, d)


def hybrid_layer(x, g_mix, w_in, w_proj_a, w_proj_b, w_out, g_ffn,
                 w_group, b_group, w_expert, b_expert, w1, w3, w2):
    b, s, _ = x.shape
    h = rms_norm(x, g_mix)
    qa, ka, va, qb, kb, vb, qi, ki, wi, ga, gb = jnp.split(h @ w_in, SPLIT_POINTS, axis=-1)

    def heads(t, n):
        return t.reshape(b, s, n, HEAD_DIM)

    qa = rope(heads(qa, MOBA_HEADS)).transpose(0, 2, 1, 3)
    ka = rope(heads(ka, MOBA_HEADS)).transpose(0, 2, 1, 3)
    va = heads(va, MOBA_HEADS).transpose(0, 2, 1, 3)
    qb = rope(heads(qb, DSA_HEADS)).transpose(0, 2, 1, 3)
    kb = rope(heads(kb, DSA_HEADS)).transpose(0, 2, 1, 3)
    vb = heads(vb, DSA_HEADS).transpose(0, 2, 1, 3)
    qi = rope(qi.reshape(b, s, IDX_HEADS, IDX_DIM))
    ki = rope(ki)
    wi = wi * IDX_SCALE
    o_a = lax.map(lambda a: moba_attend(*a), (qa, ka, va))
    o_b = lax.map(lambda a: dsa_attend(*a), (qb, kb, vb, qi, ki, wi))
    o_a = o_a.transpose(0, 2, 1, 3).reshape(b, s, MOBA_WIDTH)
    o_b = o_b.transpose(0, 2, 1, 3).reshape(b, s, DSA_WIDTH)
    mixed = jax.nn.sigmoid(ga) * (o_a @ w_proj_a) + jax.nn.sigmoid(gb) * (o_b @ w_proj_b)
    x = x + mixed @ w_out
    x = x + hier_moe(rms_norm(x, g_ffn), w_group, b_group, w_expert, b_expert, w1, w3, w2)
    return x


def setup_inputs(seed: int = 0) -> dict:
    key = jax.random.key(seed)
    ks = jax.random.split(key, 16)
    nrm = jax.random.normal
    f32 = jnp.float32
    return {
        "x": nrm(ks[0], (BATCH, SEQ, D_MODEL), f32),
        "g_mix": 1.0 + 0.02 * nrm(ks[1], (DEPTH, D_MODEL), f32),
        "w_in": nrm(ks[2], (DEPTH, D_MODEL, IN_COLS), f32) * D_MODEL ** -0.5,
        "w_proj_a": nrm(ks[3], (DEPTH, MOBA_WIDTH, D_MODEL), f32) * MOBA_WIDTH ** -0.5,
        "w_proj_b": nrm(ks[4], (DEPTH, DSA_WIDTH, D_MODEL), f32) * DSA_WIDTH ** -0.5,
        "w_out": nrm(ks[5], (DEPTH, D_MODEL, D_MODEL), f32) * D_MODEL ** -0.5,
        "g_ffn": 1.0 + 0.02 * nrm(ks[6], (DEPTH, D_MODEL), f32),
        "w_group": nrm(ks[7], (DEPTH, D_MODEL, N_GROUPS), f32) * D_MODEL ** -0.5,
        "b_group": 0.01 * nrm(ks[8], (DEPTH, N_GROUPS), f32),
        "w_expert": nrm(ks[9], (DEPTH, D_MODEL, N_EXPERTS), f32) * D_MODEL ** -0.5,
        "b_expert": 0.01 * nrm(ks[10], (DEPTH, N_EXPERTS), f32),
        "w1": nrm(ks[11], (DEPTH, N_EXPERTS, D_MODEL, EXPERT_FF), f32) * D_MODEL ** -0.5,
        "w3": nrm(ks[12], (DEPTH, N_EXPERTS, D_MODEL, EXPERT_FF), f32) * D_MODEL ** -0.5,
        "w2": nrm(ks[13], (DEPTH, N_EXPERTS, EXPERT_FF, D_MODEL), f32) * EXPERT_FF ** -0.5,
        "g_final": 1.0 + 0.02 * nrm(ks[14], (D_MODEL,), f32),
    }


def reference(x, g_mix, w_in, w_proj_a, w_proj_b, w_out, g_ffn,
              w_group, b_group, w_expert, b_expert, w1, w3, w2, g_final):
    for l in range(DEPTH):
        x = hybrid_layer(x, g_mix[l], w_in[l], w_proj_a[l], w_proj_b[l], w_out[l], g_ffn[l],
                         w_group[l], b_group[l], w_expert[l], b_expert[l], w1[l], w3[l], w2[l])
    return rms_norm(x, g_final)
```

```python
import functools

import jax
import jax.numpy as jnp
from jax import lax
from jax.experimental import pallas as pl
from jax.experimental.pallas import tpu as pltpu

F32 = jnp.float32
BF16 = jnp.bfloat16
I32 = jnp.int32

D_MODEL = 1024
HEAD_DIM = 64
N_HEADS = 8
WIDTH = N_HEADS * HEAD_DIM
N_PAIRS = N_HEADS // 2
MOBA_BLOCK = 256
MOBA_TOPK = 3
DSA_TOPK = 256
IDX_SCALE = float(WIDTH) ** -0.5
ATTN_SCALE = float(HEAD_DIM) ** -0.5
N_GROUPS = 4
EXPERTS_PER_GROUP = 8
N_EXPERTS = N_GROUPS * EXPERTS_PER_GROUP
EXPERT_FF = 512
DISPATCH_BLOCK = 256
ROPE_THETA = 10000.0
RMS_EPS = 1e-6
NEG = -1e30

LANES = 128
VMEM_LIMIT = 56 * 1024 * 1024

C_QA, C_KA, C_VA, C_QB, C_KB, C_VB, C_QI = (i * WIDTH for i in range(7))
C_KI = 7 * WIDTH
C_WI = C_KI + LANES
C_GA = C_WI + LANES
C_GB = C_GA + D_MODEL
IN_COLS_PADDED = C_GB + D_MODEL


def _cparams(*semantics):
    return pltpu.CompilerParams(dimension_semantics=semantics, vmem_limit_bytes=VMEM_LIMIT)


def _dot_nt(a, b):
    return lax.dot_general(a, b, (((1,), (1,)), ((), ())), preferred_element_type=F32)


def _dot(a, b):
    return jnp.dot(a, b, preferred_element_type=F32)


IN_TM = 512


def _in_proj_kernel(x_ref, g_ref, w_ref, cos_ref, sin_ref,
                    qa_ref, ka_ref, va_ref, qb_ref, kb_ref, vb_ref, qi_ref, ki_ref, wi_ref,
                    sga_ref, sgb_ref, kmean_ref):
    x = x_ref[...]
    h = x * lax.rsqrt(jnp.mean(x * x, axis=-1, keepdims=True) + RMS_EPS) * g_ref[...]
    hb = h.astype(BF16)
    cos = cos_ref[...]
    sin = sin_ref[...]
    upper_half = (lax.broadcasted_iota(I32, (IN_TM, LANES), 1) & (HEAD_DIM // 2)) != 0

    def rope(v):
        partner = jnp.where(upper_half, pltpu.roll(v, HEAD_DIM // 2, 1), pltpu.roll(v, LANES - HEAD_DIM // 2, 1))
        return v * cos + partner * sin

    def proj(c0, width):
        return _dot(hb, w_ref[:, c0:c0 + width])

    def store_heads(ref, c0, rotary):
        r = proj(c0, WIDTH)
        for j in range(WIDTH // LANES):
            v = r[:, j * LANES:(j + 1) * LANES]
            ref[:, j * LANES:(j + 1) * LANES] = (rope(v) if rotary else v).astype(ref.dtype)

    store_heads(qa_ref, C_QA, True)
    store_heads(va_ref, C_VA, False)
    store_heads(qb_ref, C_QB, True)
    store_heads(kb_ref, C_KB, True)
    store_heads(vb_ref, C_VB, False)
    store_heads(qi_ref, C_QI, True)

    r = proj(C_KA, WIDTH)
    for j in range(WIDTH // LANES):
        v = rope(r[:, j * LANES:(j + 1) * LANES])
        ka_ref[:, j * LANES:(j + 1) * LANES] = v.astype(BF16)
        for blk in range(IN_TM // MOBA_BLOCK):
            kmean_ref[blk, :, j * LANES:(j + 1) * LANES] = jnp.mean(
                v[blk * MOBA_BLOCK:(blk + 1) * MOBA_BLOCK], axis=0, keepdims=True)

    ki_ref[...] = rope(proj(C_KI, LANES)).astype(BF16)
    wi_ref[...] = proj(C_WI, LANES) * IDX_SCALE
    sga_ref[...] = (1.0 / (1.0 + jnp.exp(-proj(C_GA, D_MODEL)))).astype(BF16)
    sgb_ref[...] = (1.0 / (1.0 + jnp.exp(-proj(C_GB, D_MODEL)))).astype(BF16)


def _in_proj(x2, g_mix, w_r, cos_t, sin_t, seq):
    t = x2.shape[0]
    n_tiles = t // IN_TM
    tiles_per_seq = seq // IN_TM
    row = lambda i: (i, 0)
    const = lambda i: (0, 0)
    act = lambda w, dt: jax.ShapeDtypeStruct((t, w), dt)
    out_shape = ([act(WIDTH, BF16)] * 7 + [act(LANES, BF16), act(LANES, F32), act(D_MODEL, BF16), act(D_MODEL, BF16),
                                           jax.ShapeDtypeStruct((t // MOBA_BLOCK, 1, WIDTH), F32)])
    out_specs = ([pl.BlockSpec((IN_TM, WIDTH), row)] * 7
                 + [pl.BlockSpec((IN_TM, LANES), row), pl.BlockSpec((IN_TM, LANES), row),
                    pl.BlockSpec((IN_TM, D_MODEL), row), pl.BlockSpec((IN_TM, D_MODEL), row),
                    pl.BlockSpec((IN_TM // MOBA_BLOCK, 1, WIDTH), lambda i: (i, 0, 0))])
    return pl.pallas_call(
        _in_proj_kernel,
        grid=(n_tiles,),
        in_specs=[pl.BlockSpec((IN_TM, D_MODEL), row),
                  pl.BlockSpec((1, D_MODEL), const),
                  pl.BlockSpec((D_MODEL, IN_COLS_PADDED), const, pipeline_mode=pl.Buffered(1)),
                  pl.BlockSpec((IN_TM, LANES), lambda i: (i % tiles_per_seq, 0)),
                  pl.BlockSpec((IN_TM, LANES), lambda i: (i % tiles_per_seq, 0))],
        out_specs=out_specs,
        out_shape=out_shape,
        compiler_params=_cparams("parallel"),
        name="in_proj",
    )(x2, g_mix, w_r, cos_t, sin_t)


def _moba_select_kernel(qa_ref, km_ref, selb_ref):
    own = pl.program_id(1)
    n_blocks = km_ref.shape[1]
    km = km_ref[0]
    col_head = lax.broadcasted_iota(I32, (N_HEADS, WIDTH), 1) // HEAD_DIM
    head_mask = col_head == lax.broadcasted_iota(I32, (N_HEADS, WIDTH), 0)
    rows = [jnp.where(head_mask, km[n:n + 1, :], 0.0) for n in range(n_blocks)]
    rows.append(jnp.zeros((LANES - n_blocks * N_HEADS, WIDTH), F32))
    km_t = jnp.concatenate(rows, axis=0).astype(BF16)
    gate = _dot_nt(qa_ref[...], km_t)
    lane = lax.broadcasted_iota(I32, gate.shape, 1)
    past = (lane // N_HEADS) < own
    g = jnp.where(past, gate, NEG)
    rank = jnp.zeros(gate.shape, I32)
    for r in range(1, n_blocks):
        later = pltpu.roll(g, LANES - N_HEADS * r, 1)
        earlier = pltpu.roll(g, N_HEADS * r, 1)
        rank = rank + (later > g).astype(I32) + (earlier >= g).astype(I32)
    chosen = jnp.where(past & (rank < MOBA_TOPK), 1.0, 0.0).astype(BF16)
    src = lax.broadcasted_iota(I32, (LANES, N_PAIRS * LANES), 0)
    dst = lax.broadcasted_iota(I32, (LANES, N_PAIRS * LANES), 1)
    src_n, src_h = src // N_HEADS, src % N_HEADS
    dst_pair, dst_w = dst // LANES, dst % LANES
    expand = ((src_n < n_blocks) & (dst_w < 2 * N_HEADS) & (dst_w % N_HEADS == src_n)
              & (dst_pair * 2 + dst_w // N_HEADS == src_h))
    hit = _dot(chosen, jnp.where(expand, 1.0, 0.0).astype(BF16))
    selb_ref[...] = jnp.where(hit > 0.5, 0.0, NEG)


def _moba_select(qa, kmean, batch, seq):
    n_blocks = seq // MOBA_BLOCK
    assert n_blocks * N_HEADS <= LANES and n_blocks <= N_HEADS
    t = batch * seq
    return pl.pallas_call(
        _moba_select_kernel,
        grid=(batch, n_blocks),
        in_specs=[pl.BlockSpec((MOBA_BLOCK, WIDTH), lambda b, i: (b * n_blocks + i, 0)),
                  pl.BlockSpec((1, n_blocks, WIDTH), lambda b, i: (b, 0, 0))],
        out_specs=pl.BlockSpec((MOBA_BLOCK, N_PAIRS * LANES), lambda b, i: (b * n_blocks + i, 0)),
        out_shape=jax.ShapeDtypeStruct((t, N_PAIRS * LANES), F32),
        compiler_params=_cparams("parallel", "parallel"),
        name="moba_select",
    )(qa, kmean.reshape(batch, n_blocks, WIDTH))


def _stack_heads(q2):
    lane = lax.broadcasted_iota(I32, q2.shape, 1)
    zero = jnp.zeros_like(q2)
    return jnp.concatenate([jnp.where(lane < HEAD_DIM, q2, zero), jnp.where(lane >= HEAD_DIM, q2, zero)], axis=0)


def _unstack_heads(o):
    rows = o.shape[0] // 2
    lane = lax.broadcasted_iota(I32, (rows, LANES), 1)
    return jnp.where(lane < HEAD_DIM, o[:rows], o[rows:])


def _softmax_step(s, v, m_ref, l_ref, acc_ref, first):
    if first:
        m_new = jnp.max(s, axis=-1, keepdims=True)
        p = jnp.exp(s - m_new)
        l_ref[...] = jnp.sum(p, axis=-1, keepdims=True)
        acc_ref[...] = _dot(p.astype(BF16), v)
    else:
        m_old = m_ref[...]
        m_new = jnp.maximum(m_old, jnp.max(s, axis=-1, keepdims=True))
        alpha = jnp.exp(m_old - m_new)
        p = jnp.exp(s - m_new)
        l_ref[...] = alpha * l_ref[...] + jnp.sum(p, axis=-1, keepdims=True)
        acc_ref[...] = alpha * acc_ref[...] + _dot(p.astype(BF16), v)
    m_ref[...] = m_new


def _moba_attn_kernel(q_ref, k_ref, v_ref, selb_ref, o_ref, m_ref, l_ref, acc_ref):
    own = pl.program_id(2)
    n_blocks = k_ref.shape[0] // MOBA_BLOCK
    qs = _stack_heads(q_ref[...])
    own_start = pl.multiple_of(own * MOBA_BLOCK, MOBA_BLOCK)
    s = _dot_nt(qs, k_ref[pl.ds(own_start, MOBA_BLOCK), :])
    q_pos = lax.broadcasted_iota(I32, s.shape, 0) % MOBA_BLOCK
    k_pos = lax.broadcasted_iota(I32, s.shape, 1)
    s = jnp.where(k_pos <= q_pos, s, NEG)
    _softmax_step(s, v_ref[pl.ds(own_start, MOBA_BLOCK), :], m_ref, l_ref, acc_ref, first=True)
    selb = selb_ref[...]
    for j in range(n_blocks - 1):
        @pl.when(j < own)
        def _(j=j):
            bias = jnp.concatenate([selb[:, j:j + 1], selb[:, N_HEADS + j:N_HEADS + j + 1]], axis=0)
            s = _dot_nt(qs, k_ref[j * MOBA_BLOCK:(j + 1) * MOBA_BLOCK, :]) + bias
            _softmax_step(s, v_ref[j * MOBA_BLOCK:(j + 1) * MOBA_BLOCK, :], m_ref, l_ref, acc_ref, first=False)
    o_ref[...] = _unstack_heads(acc_ref[...] / l_ref[...]).astype(o_ref.dtype)


def _moba_attn(qa, ka, va, selb, batch, seq):
    n_blocks = seq // MOBA_BLOCK
    t = batch * seq
    q_map = lambda b, p, i: (b * n_blocks + i, p)
    kv_map = lambda b, p, i: (b, p)
    return pl.pallas_call(
        _moba_attn_kernel,
        grid=(batch, N_PAIRS, n_blocks),
        in_specs=[pl.BlockSpec((MOBA_BLOCK, LANES), q_map),
                  pl.BlockSpec((seq, LANES), kv_map),
                  pl.BlockSpec((seq, LANES), kv_map),
                  pl.BlockSpec((MOBA_BLOCK, LANES), q_map)],
        out_specs=pl.BlockSpec((MOBA_BLOCK, LANES), q_map),
        out_shape=jax.ShapeDtypeStruct((t, WIDTH), BF16),
        scratch_shapes=[pltpu.VMEM((2 * MOBA_BLOCK, 1), F32), pltpu.VMEM((2 * MOBA_BLOCK, 1), F32),
                        pltpu.VMEM((2 * MOBA_BLOCK, LANES), F32)],
        compiler_params=_cparams("parallel", "parallel", "arbitrary"),
        name="moba_attn",
    )(qa, ka, va, selb)


def _rope_tables(seq):
    inv = jnp.power(ROPE_THETA, -jnp.arange(0, HEAD_DIM, 2, dtype=F32) / HEAD_DIM)
    ang = jnp.arange(seq, dtype=F32)[:, None] * inv[None, :]
    cos, sin = jnp.cos(ang), jnp.sin(ang)
    reps = LANES // HEAD_DIM
    return jnp.tile(jnp.concatenate([cos, cos], axis=-1), (1, reps)), jnp.tile(jnp.concatenate([-sin, sin], axis=-1), (1, reps))


def _arrange_w_in(w_in):
    sizes = (WIDTH,) * 7 + (HEAD_DIM, N_HEADS, D_MODEL, D_MODEL)
    offs = [0]
    for s in sizes:
        offs.append(offs[-1] + s)
    seg = [w_in[:, offs[i]:offs[i + 1]] for i in range(len(sizes))]
    qa, ka, va, qb, kb, vb, qi, ki, wi, ga, gb = seg
    wi_pad = jnp.pad(wi, ((0, 0), (0, LANES - N_HEADS)))
    w = jnp.concatenate([qa * ATTN_SCALE, ka, va, qb * ATTN_SCALE, kb, vb, qi, ki, ki, wi_pad, ga, gb], axis=1)
    return w.astype(BF16)


DSA_TQ = 256
DSA_TK = 256
DSA_ROWS = 64
INT_MIN = -(2 ** 31)


def _dsa_kernel(qi_ref, ki_ref, wi_ref, q_ref, k_ref, v_ref, o_ref, key_ref, bias_ref, m_ref, l_ref, acc_ref):
    c = pl.program_id(1)
    n_tiles = k_ref.shape[0] // DSA_TK
    n_live = c + 1
    row = lax.broadcasted_iota(I32, (DSA_TQ, DSA_TK), 0)
    col = lax.broadcasted_iota(I32, (DSA_TQ, DSA_TK), 1)

    qi = qi_ref[...]
    wi = wi_ref[...]
    qis = [_stack_heads(qi[:, p * LANES:(p + 1) * LANES]) for p in range(N_PAIRS)]
    for t in range(n_tiles):
        @pl.when(t <= c)
        def _(t=t):
            kt = ki_ref[t * DSA_TK:(t + 1) * DSA_TK, :]
            score = jnp.zeros((DSA_TQ, DSA_TK), F32)
            for p in range(N_PAIRS):
                logit = _dot_nt(qis[p], kt)
                score = score + wi[:, 2 * p:2 * p + 1] * jnp.maximum(logit[:DSA_TQ], 0.0)
                score = score + wi[:, 2 * p + 1:2 * p + 2] * jnp.maximum(logit[DSA_TQ:], 0.0)
            score = jnp.where(t * DSA_TK + col <= c * DSA_TQ + row, score, NEG)
            bits = pltpu.bitcast(score, I32)
            key = bits ^ ((bits >> 31) & 0x7FFFFFFF)
            key_ref[t] = jnp.where(score == 0.0, 0, key)

    def count_ge(rg, cand):
        def body(t, acc):
            k = key_ref[t, rg * DSA_ROWS:(rg + 1) * DSA_ROWS, :]
            hit = jnp.where(k >= cand, 1.0, 0.0)
            return acc + hit[:, :LANES] + hit[:, LANES:]
        acc = lax.fori_loop(0, n_live, body, jnp.zeros((DSA_ROWS, LANES), F32))
        return jnp.sum(acc, axis=-1, keepdims=True)

    n_rg = DSA_TQ // DSA_ROWS

    def first_bit(rg):
        cand = jnp.zeros((DSA_ROWS, 1), I32)
        return jnp.where(count_ge(rg, cand) >= DSA_TOPK, cand, INT_MIN)

    def bit_body(ib, ts):
        bit = jnp.left_shift(jnp.int32(1), 30 - ib)
        out = []
        for rg in range(n_rg):
            cand = ts[rg] | bit
            out.append(jnp.where(count_ge(rg, cand) >= DSA_TOPK, cand, ts[rg]))
        return tuple(out)

    ts = lax.fori_loop(0, 31, bit_body, tuple(first_bit(rg) for rg in range(n_rg)))
    thr = jnp.concatenate(ts, axis=0)

    def above_body(t, acc):
        hit = jnp.where(key_ref[t] > thr, 1.0, 0.0)
        return acc + hit[:, :LANES] + hit[:, LANES:]
    n_above = jnp.sum(lax.fori_loop(0, n_live, above_body, jnp.zeros((DSA_TQ, LANES), F32)), axis=-1, keepdims=True)
    need = float(DSA_TOPK) - n_above
    strictly_before = jnp.where(row < col, 1.0, 0.0).astype(BF16)

    def mask_body(t, carry):
        k = key_ref[t]
        tie = jnp.where(k == thr, 1.0, 0.0)
        ties_before = carry + _dot(tie.astype(BF16), strictly_before)
        chosen = (k > thr) | ((k == thr) & (ties_before < need))
        chosen = chosen & (t * DSA_TK + col <= c * DSA_TQ + row)
        bias_ref[t] = jnp.where(chosen, 0.0, NEG)
        return carry + jnp.sum(tie, axis=-1, keepdims=True)
    lax.fori_loop(0, n_live, mask_body, jnp.zeros((DSA_TQ, 1), F32))

    for p in range(N_PAIRS):
        qs = _stack_heads(q_ref[:, p * LANES:(p + 1) * LANES])
        for t in range(n_tiles):
            def tile(t=t, p=p, qs=qs):
                bias = bias_ref[t]
                s = _dot_nt(qs, k_ref[t * DSA_TK:(t + 1) * DSA_TK, p * LANES:(p + 1) * LANES])
                s = s + jnp.concatenate([bias, bias], axis=0)
                _softmax_step(s, v_ref[t * DSA_TK:(t + 1) * DSA_TK, p * LANES:(p + 1) * LANES],
                              m_ref, l_ref, acc_ref, first=(t == 0))
            if t == 0:
                tile()
            else:
                pl.when(t <= c)(tile)
        o_ref[:, p * LANES:(p + 1) * LANES] = _unstack_heads(acc_ref[...] / l_ref[...]).astype(o_ref.dtype)


def _dsa_attn(qi, ki2, wi, qb, kb, vb, batch, seq):
    assert DSA_TQ == DSA_TK and seq % DSA_TQ == 0 and min(DSA_TOPK, seq // 4) == DSA_TOPK
    n_chunks = seq // DSA_TQ
    n_tiles = seq // DSA_TK
    t = batch * seq
    q_map = lambda b, c: (b * n_chunks + c, 0)
    kv_map = lambda b, c: (b, 0)
    return pl.pallas_call(
        _dsa_kernel,
        grid=(batch, n_chunks),
        in_specs=[pl.BlockSpec((DSA_TQ, WIDTH), q_map),
                  pl.BlockSpec((seq, LANES), kv_map),
                  pl.BlockSpec((DSA_TQ, LANES), q_map),
                  pl.BlockSpec((DSA_TQ, WIDTH), q_map),
                  pl.BlockSpec((seq, WIDTH), kv_map),
                  pl.BlockSpec((seq, WIDTH), kv_map)],
        out_specs=pl.BlockSpec((DSA_TQ, WIDTH), q_map),
        out_shape=jax.ShapeDtypeStruct((t, WIDTH), BF16),
        scratch_shapes=[pltpu.VMEM((n_tiles, DSA_TQ, DSA_TK), I32), pltpu.VMEM((n_tiles, DSA_TQ, DSA_TK), F32),
                        pltpu.VMEM((2 * DSA_TQ, 1), F32), pltpu.VMEM((2 * DSA_TQ, 1), F32),
                        pltpu.VMEM((2 * DSA_TQ, LANES), F32)],
        compiler_params=_cparams("parallel", "arbitrary"),
        name="dsa_attn",
    )(qi, ki2, wi, qb, kb, vb)


MIX_TM = 512
ROUTE_ROWS = 8
XS_WORDS = D_MODEL // 2


def _pack_bf16_pair(x):
    n = x.shape[1] // 2
    bits = pltpu.bitcast(x.astype(BF16).astype(F32), jnp.uint32)
    return (bits[:, n:] & jnp.uint32(0xFFFF0000)) | (bits[:, :n] >> 16)


def _unpack_bf16_pair(w):
    lo = pltpu.bitcast(w << 16, F32)
    hi = pltpu.bitcast(w & jnp.uint32(0xFFFF0000), F32)
    return jnp.concatenate([lo, hi], axis=1)


def _mix_out_kernel(x_ref, oa_ref, ob_ref, sga_ref, sgb_ref, wa_ref, wb_ref, wo_ref, g_ref, wr_hi_ref, wr_lo_ref, br_ref,
                    x1_ref, xs_ref, route_ref, route_t_ref):
    mixed = (sga_ref[...].astype(F32) * _dot(oa_ref[...], wa_ref[...])
             + sgb_ref[...].astype(F32) * _dot(ob_ref[...], wb_ref[...]))
    x1 = x_ref[...] + _dot(mixed.astype(BF16), wo_ref[...])
    x1_ref[...] = x1
    hn = x1 * lax.rsqrt(jnp.mean(x1 * x1, axis=-1, keepdims=True) + RMS_EPS) * g_ref[...]
    xs_ref[...] = _pack_bf16_pair(hn)

    hi = hn.astype(BF16)
    lo = (hn - hi.astype(F32)).astype(BF16)
    logits = _dot(hi, wr_hi_ref[...]) + _dot(lo, wr_hi_ref[...]) + _dot(hi, wr_lo_ref[...]) + br_ref[...]
    lane = lax.broadcasted_iota(I32, logits.shape, 1)
    far = jnp.int32(LANES)

    def first_lane_of_max(v, valid):
        top = jnp.max(jnp.where(valid, v, NEG), axis=-1, keepdims=True)
        return top, jnp.min(jnp.where(valid & (v == top), lane, far), axis=-1, keepdims=True)

    is_group = lane < N_GROUPS
    g_max, g_sel = first_lane_of_max(logits, is_group)
    g_w = 1.0 / jnp.sum(jnp.where(is_group, jnp.exp(logits - g_max), 0.0), axis=-1, keepdims=True)
    first = N_GROUPS + g_sel * EXPERTS_PER_GROUP
    in_group = (lane >= first) & (lane < first + EXPERTS_PER_GROUP)
    e_max, _ = first_lane_of_max(logits, in_group)
    e_exp = jnp.where(in_group, jnp.exp(logits - e_max), 0.0)
    prob = e_exp / jnp.sum(e_exp, axis=-1, keepdims=True)
    p1, i1 = first_lane_of_max(prob, in_group)
    p2, i2 = first_lane_of_max(prob, in_group & (lane != i1))
    denom = p1 + p2
    record = jnp.where(lane == 0, (i1 - N_GROUPS).astype(F32),
                       jnp.where(lane == 1, (i2 - N_GROUPS).astype(F32),
                                 jnp.where(lane == 2, g_w * p1 / denom,
                                           jnp.where(lane == 3, g_w * p2 / denom, 0.0))))
    route_ref[...] = record
    route_t_ref[...] = record.T[:ROUTE_ROWS, :]


def _mix_out(x2, oa, ob, sga, sgb, wa, wb, wo, g_ffn, wr_hi, wr_lo, br):
    t = x2.shape[0]
    row = lambda i: (i, 0)
    const = lambda i: (0, 0)
    once = dict(pipeline_mode=pl.Buffered(1))
    return pl.pallas_call(
        _mix_out_kernel,
        grid=(t // MIX_TM,),
        in_specs=[pl.BlockSpec((MIX_TM, D_MODEL), row),
                  pl.BlockSpec((MIX_TM, WIDTH), row), pl.BlockSpec((MIX_TM, WIDTH), row),
                  pl.BlockSpec((MIX_TM, D_MODEL), row), pl.BlockSpec((MIX_TM, D_MODEL), row),
                  pl.BlockSpec((WIDTH, D_MODEL), const, **once), pl.BlockSpec((WIDTH, D_MODEL), const, **once),
                  pl.BlockSpec((D_MODEL, D_MODEL), const, **once), pl.BlockSpec((1, D_MODEL), const),
                  pl.BlockSpec((D_MODEL, LANES), const, **once), pl.BlockSpec((D_MODEL, LANES), const, **once),
                  pl.BlockSpec((1, LANES), const)],
        out_specs=[pl.BlockSpec((MIX_TM, D_MODEL), row), pl.BlockSpec((MIX_TM, XS_WORDS), row),
                   pl.BlockSpec((MIX_TM, LANES), row), pl.BlockSpec((ROUTE_ROWS, MIX_TM), lambda i: (0, i))],
        out_shape=[jax.ShapeDtypeStruct((t, D_MODEL), F32), jax.ShapeDtypeStruct((t, XS_WORDS), jnp.uint32),
                   jax.ShapeDtypeStruct((t, LANES), F32), jax.ShapeDtypeStruct((ROUTE_ROWS, t), F32)],
        compiler_params=_cparams("parallel"),
        name="mix_out",
    )(x2, oa, ob, sga, sgb, wa, wb, wo, g_ffn, wr_hi, wr_lo, br)


PLAN_TM = 512


def _moe_plan_kernel(route_t_ref, dest_ref, counts_ref, count_ref, start_ref):
    phase = pl.program_id(0)
    step = pl.program_id(1)
    expert = lax.broadcasted_iota(I32, (LANES, PLAN_TM), 0)
    e1 = route_t_ref[0:1, :].astype(I32)
    e2 = route_t_ref[1:2, :].astype(I32)
    hot1 = expert == e1
    hot2 = expert == e2
    hot = jnp.where(hot1 | hot2, 1.0, 0.0)

    @pl.when((phase == 0) & (step == 0))
    def _():
        count_ref[...] = jnp.zeros_like(count_ref)

    @pl.when(phase == 0)
    def _():
        count_ref[...] += jnp.sum(hot, axis=-1, keepdims=True)
        dest_ref[...] = jnp.zeros_like(dest_ref)

    @pl.when((phase == 1) & (step == 0))
    def _():
        counts = jnp.broadcast_to(count_ref[...], (LANES, LANES))
        counts_ref[...] = counts
        padded = jnp.ceil(counts / DISPATCH_BLOCK) * DISPATCH_BLOCK
        sub = lax.broadcasted_iota(I32, (LANES, LANES), 0)
        ends = padded
        shift = 1
        while shift < LANES:
            ends = ends + jnp.where(sub >= shift, pltpu.roll(ends, shift, 0), 0.0)
            shift *= 2
        start_ref[...] = (ends - padded)[:, 0:1]
        count_ref[...] = jnp.zeros_like(count_ref)

    @pl.when(phase == 1)
    def _():
        tok_r = lax.broadcasted_iota(I32, (PLAN_TM, PLAN_TM), 0)
        tok_c = lax.broadcasted_iota(I32, (PLAN_TM, PLAN_TM), 1)
        earlier = jnp.where(tok_r < tok_c, 1.0, 0.0).astype(BF16)
        slot = start_ref[...] + count_ref[...] + _dot(hot.astype(BF16), earlier)
        d1 = jnp.sum(jnp.where(hot1, slot, 0.0), axis=0, keepdims=True)
        d2 = jnp.sum(jnp.where(hot2, slot, 0.0), axis=0, keepdims=True)
        sub = lax.broadcasted_iota(I32, (ROUTE_ROWS, PLAN_TM), 0)
        dest_ref[...] = jnp.where(sub == 0, d1, jnp.where(sub == 1, d2, 0.0)).astype(I32)
        count_ref[...] += jnp.sum(hot, axis=-1, keepdims=True)


def _moe_plan(route_t):
    t = route_t.shape[1]
    return pl.pallas_call(
        _moe_plan_kernel,
        grid=(2, t // PLAN_TM),
        in_specs=[pl.BlockSpec((ROUTE_ROWS, PLAN_TM), lambda ph, i: (0, i))],
        out_specs=[pl.BlockSpec((ROUTE_ROWS, PLAN_TM), lambda ph, i: (0, i * ph)),
                   pl.BlockSpec((LANES, LANES), lambda ph, i: (0, 0))],
        out_shape=[jax.ShapeDtypeStruct((ROUTE_ROWS, t), I32), jax.ShapeDtypeStruct((LANES, LANES), F32)],
        scratch_shapes=[pltpu.VMEM((LANES, 1), F32), pltpu.VMEM((LANES, 1), F32)],
        compiler_params=_cparams("arbitrary", "arbitrary"),
        name="moe_plan",
    )(route_t)


DISP_TM = 1024


def _moe_dispatch_kernel(d1_ref, d2_ref, xs_ref, zeros_ref, out_ref, sem):
    del zeros_ref

    def row_copy(r, dest):
        return pltpu.make_async_copy(xs_ref.at[pl.ds(r, 1), :], out_ref.at[pl.ds(dest, 1), :], sem)

    def issue(r, carry):
        row_copy(r, d1_ref[r]).start()
        row_copy(r, d2_ref[r]).start()
        return carry
    lax.fori_loop(0, DISP_TM, issue, 0)

    for _ in range(2):
        pltpu.make_async_copy(xs_ref, out_ref.at[pl.ds(0, DISP_TM), :], sem).wait()


def _moe_dispatch(d1, d2, xs, n_rows):
    t = xs.shape[0]
    smem = lambda: pl.BlockSpec((DISP_TM,), lambda i: (i,), memory_space=pltpu.SMEM)
    return pl.pallas_call(
        _moe_dispatch_kernel,
        grid=(t // DISP_TM,),
        in_specs=[smem(), smem(), pl.BlockSpec((DISP_TM, XS_WORDS), lambda i: (i, 0)),
                  pl.BlockSpec(memory_space=pl.ANY)],
        out_specs=pl.BlockSpec(memory_space=pl.ANY),
        out_shape=jax.ShapeDtypeStruct((n_rows, XS_WORDS), jnp.uint32),
        scratch_shapes=[pltpu.SemaphoreType.DMA(())],
        input_output_aliases={3: 0},
        compiler_params=_cparams("arbitrary"),
        name="moe_dispatch",
    )(d1, d2, xs, jnp.zeros((n_rows, XS_WORDS), jnp.uint32))


def _moe_expert_kernel(block_expert_ref, n_used_ref, xs_ref, w1_ref, w3_ref, w2_ref, ys_ref):
    del block_expert_ref
    live = pl.program_id(0) < n_used_ref[0]

    @pl.when(live)
    def _():
        x = _unpack_bf16_pair(xs_ref[...]).astype(BF16)
        h1 = _dot(x, w1_ref[0])
        h3 = _dot(x, w3_ref[0])
        hid = h1 / (1.0 + jnp.exp(-h1)) * h3
        ys_ref[...] = _dot(hid.astype(BF16), w2_ref[0])

    @pl.when(jnp.logical_not(live))
    def _():
        ys_ref[...] = jnp.zeros_like(ys_ref)


def _moe_experts(block_expert, n_used, xs_sorted, w1, w3, w2):
    n_rows = xs_sorted.shape[0]
    grid_spec = pltpu.PrefetchScalarGridSpec(
        num_scalar_prefetch=2,
        grid=(n_rows // DISPATCH_BLOCK,),
        in_specs=[pl.BlockSpec((DISPATCH_BLOCK, XS_WORDS), lambda j, be, nu: (j, 0)),
                  pl.BlockSpec((1, D_MODEL, EXPERT_FF), lambda j, be, nu: (be[j], 0, 0)),
                  pl.BlockSpec((1, D_MODEL, EXPERT_FF), lambda j, be, nu: (be[j], 0, 0)),
                  pl.BlockSpec((1, EXPERT_FF, D_MODEL), lambda j, be, nu: (be[j], 0, 0))],
        out_specs=pl.BlockSpec((DISPATCH_BLOCK, D_MODEL), lambda j, be, nu: (j, 0)))
    return pl.pallas_call(
        _moe_expert_kernel,
        grid_spec=grid_spec,
        out_shape=jax.ShapeDtypeStruct((n_rows, D_MODEL), F32),
        compiler_params=_cparams("arbitrary"),
        name="moe_experts",
    )(block_expert, n_used, xs_sorted, w1, w3, w2)


COMB_TM = 512


def _moe_combine_kernel(d1_ref, d2_ref, x1_ref, route_ref, g_ref, ys_ref, out_ref, y1_ref, y2_ref, sem):
    def row_copy(src, r, buf):
        return pltpu.make_async_copy(ys_ref.at[pl.ds(src, 1), :], buf.at[pl.ds(r, 1), :], sem)

    def issue(r, carry):
        row_copy(d1_ref[r], r, y1_ref).start()
        row_copy(d2_ref[r], r, y2_ref).start()
        return carry
    lax.fori_loop(0, COMB_TM, issue, 0)

    for buf in (y1_ref, y2_ref):
        pltpu.make_async_copy(ys_ref.at[pl.ds(0, COMB_TM), :], buf, sem).wait()

    route = route_ref[...]
    x2 = x1_ref[...] + (route[:, 2:3] * y1_ref[...] + route[:, 3:4] * y2_ref[...])
    out_ref[...] = x2 * lax.rsqrt(jnp.mean(x2 * x2, axis=-1, keepdims=True) + RMS_EPS) * g_ref[...]


def _moe_combine(d1, d2, x1, route, g_final, ys):
    t = x1.shape[0]
    smem = lambda: pl.BlockSpec((COMB_TM,), lambda i: (i,), memory_space=pltpu.SMEM)
    row = lambda i: (i, 0)
    return pl.pallas_call(
        _moe_combine_kernel,
        grid=(t // COMB_TM,),
        in_specs=[smem(), smem(), pl.BlockSpec((COMB_TM, D_MODEL), row), pl.BlockSpec((COMB_TM, LANES), row),
                  pl.BlockSpec((1, D_MODEL), lambda i: (0, 0)), pl.BlockSpec(memory_space=pl.ANY)],
        out_specs=pl.BlockSpec((COMB_TM, D_MODEL), row),
        out_shape=jax.ShapeDtypeStruct((t, D_MODEL), F32),
        scratch_shapes=[pltpu.VMEM((COMB_TM, D_MODEL), F32), pltpu.VMEM((COMB_TM, D_MODEL), F32),
                        pltpu.SemaphoreType.DMA(())],
        compiler_params=_cparams("arbitrary"),
        name="moe_combine",
    )(d1, d2, x1, route, g_final, ys)


def _hier_moe_tail(x1, xs, route, route_t, w1, w3, w2, g_final):
    t = x1.shape[0]
    dest, counts = _moe_plan(route_t)
    counts = counts[:N_EXPERTS, 0].astype(I32)
    padded = (counts + DISPATCH_BLOCK - 1) // DISPATCH_BLOCK * DISPATCH_BLOCK
    ends = jnp.cumsum(padded)
    n_blocks = (t * 2) // DISPATCH_BLOCK + N_EXPERTS
    block_expert = jnp.minimum(
        jnp.searchsorted(ends, jnp.arange(n_blocks, dtype=I32) * DISPATCH_BLOCK, side="right"), N_EXPERTS - 1).astype(I32)
    n_used = (ends[-1:] // DISPATCH_BLOCK).astype(I32)
    d1, d2 = dest[0], dest[1]
    xs_sorted = _moe_dispatch(d1, d2, xs, n_blocks * DISPATCH_BLOCK)
    ys = _moe_experts(block_expert, n_used, xs_sorted, w1, w3, w2)
    return _moe_combine(d1, d2, x1, route, g_final, ys)


def kernel(x, g_mix, w_in, w_proj_a, w_proj_b, w_out, g_ffn, w_group, b_group, w_expert, b_expert, w1, w3, w2, g_final):
    batch, seq, d = x.shape
    assert d == D_MODEL and g_mix.shape[0] == 1, "one layer of width D_MODEL"
    x2 = x.reshape(batch * seq, d)
    cos_t, sin_t = _rope_tables(seq)
    qa, ka, va, qb, kb, vb, qi, ki2, wi, sga, sgb, kmean = _in_proj(
        x2, g_mix[0][None, :], _arrange_w_in(w_in[0]), cos_t, sin_t, seq)
    selb = _moba_select(qa, kmean, batch, seq)
    oa = _moba_attn(qa, ka, va, selb, batch, seq)
    ob = _dsa_attn(qi, ki2, wi, qb, kb, vb, batch, seq)
    wr_hi, wr_lo, br = _router_params(w_group[0], b_group[0], w_expert[0], b_expert[0])
    x1, xs, route, route_t = _mix_out(
        x2, oa, ob, sga, sgb, w_proj_a[0].astype(BF16), w_proj_b[0].astype(BF16), w_out[0].astype(BF16),
        g_ffn[0][None, :], wr_hi, wr_lo, br)
    out = _hier_moe_tail(x1, xs, route, route_t, w1[0].astype(BF16), w3[0].astype(BF16), w2[0].astype(BF16),
                         g_final[None, :])
    return out.reshape(batch, seq, d)


def _router_params(w_group, b_group, w_expert, b_expert):
    w = jnp.pad(jnp.concatenate([w_group, w_expert], axis=1), ((0, 0), (0, LANES - N_GROUPS - N_EXPERTS)))
    b = jnp.pad(jnp.concatenate([b_group, b_expert]), (0, LANES - N_GROUPS - N_EXPERTS))[None, :]
    w_hi = w.astype(BF16)
    w_lo = (w - w_hi.astype(F32)).astype(BF16)
    return w_hi, w_lo, b
```

```python
import functools

import jax
import jax.numpy as jnp
from jax import lax
from jax.experimental import pallas as pl
from jax.experimental.pallas import tpu as pltpu

F32 = jnp.float32
BF16 = jnp.bfloat16
I32 = jnp.int32

D_MODEL = 1024
HEAD_DIM = 64
N_HEADS = 8
WIDTH = N_HEADS * HEAD_DIM
N_PAIRS = N_HEADS // 2
MOBA_BLOCK = 256
MOBA_TOPK = 3
DSA_TOPK = 256
IDX_SCALE = float(WIDTH) ** -0.5
ATTN_SCALE = float(HEAD_DIM) ** -0.5
N_GROUPS = 4
EXPERTS_PER_GROUP = 8
N_EXPERTS = N_GROUPS * EXPERTS_PER_GROUP
EXPERT_FF = 512
DISPATCH_BLOCK = 256
ROPE_THETA = 10000.0
RMS_EPS = 1e-6
NEG = -1e30

LANES = 128
VMEM_LIMIT = 56 * 1024 * 1024

C_QA, C_KA, C_VA, C_QB, C_KB, C_VB, C_QI = (i * WIDTH for i in range(7))
C_KI = 7 * WIDTH
C_WI = C_KI + LANES
C_GA = C_WI + LANES
C_GB = C_GA + D_MODEL
IN_COLS_PADDED = C_GB + D_MODEL


def _cparams(*semantics):
    return pltpu.CompilerParams(dimension_semantics=semantics, vmem_limit_bytes=VMEM_LIMIT)


def _dot_nt(a, b):
    return lax.dot_general(a, b, (((1,), (1,)), ((), ())), preferred_element_type=F32)


def _dot(a, b):
    return jnp.dot(a, b, preferred_element_type=F32)


IN_TM = 512


def _in_proj_kernel(x_ref, g_ref, w_ref, cos_ref, sin_ref,
                    qa_ref, ka_ref, va_ref, qb_ref, kb_ref, vb_ref, qi_ref, ki_ref, wi_ref,
                    sga_ref, sgb_ref, kmean_ref):
    x = x_ref[...]
    h = x * lax.rsqrt(jnp.mean(x * x, axis=-1, keepdims=True) + RMS_EPS) * g_ref[...]
    hb = h.astype(BF16)
    cos = cos_ref[...]
    sin = sin_ref[...]
    upper_half = (lax.broadcasted_iota(I32, (IN_TM, LANES), 1) & (HEAD_DIM // 2)) != 0

    def rope(v):
        partner = jnp.where(upper_half, pltpu.roll(v, HEAD_DIM // 2, 1), pltpu.roll(v, LANES - HEAD_DIM // 2, 1))
        return v * cos + partner * sin

    def proj(c0, width):
        return _dot(hb, w_ref[:, c0:c0 + width])

    def store_heads(ref, c0, rotary):
        r = proj(c0, WIDTH)
        for j in range(WIDTH // LANES):
            v = r[:, j * LANES:(j + 1) * LANES]
            ref[:, j * LANES:(j + 1) * LANES] = (rope(v) if rotary else v).astype(ref.dtype)

    store_heads(qa_ref, C_QA, True)
    store_heads(va_ref, C_VA, False)
    store_heads(qb_ref, C_QB, True)
    store_heads(kb_ref, C_KB, True)
    store_heads(vb_ref, C_VB, False)
    store_heads(qi_ref, C_QI, True)

    r = proj(C_KA, WIDTH)
    for j in range(WIDTH // LANES):
        v = rope(r[:, j * LANES:(j + 1) * LANES])
        ka_ref[:, j * LANES:(j + 1) * LANES] = v.astype(BF16)
        for blk in range(IN_TM // MOBA_BLOCK):
            kmean_ref[blk, :, j * LANES:(j + 1) * LANES] = jnp.mean(
                v[blk * MOBA_BLOCK:(blk + 1) * MOBA_BLOCK], axis=0, keepdims=True)

    ki_ref[...] = rope(proj(C_KI, LANES)).astype(BF16)
    wi_ref[...] = proj(C_WI, LANES) * IDX_SCALE
    sga_ref[...] = (1.0 / (1.0 + jnp.exp(-proj(C_GA, D_MODEL)))).astype(BF16)
    sgb_ref[...] = (1.0 / (1.0 + jnp.exp(-proj(C_GB, D_MODEL)))).astype(BF16)


def _in_proj(x2, g_mix, w_r, cos_t, sin_t, seq):
    t = x2.shape[0]
    n_tiles = t // IN_TM
    tiles_per_seq = seq // IN_TM
    row = lambda i: (i, 0)
    const = lambda i: (0, 0)
    act = lambda w, dt: jax.ShapeDtypeStruct((t, w), dt)
    out_shape = ([act(WIDTH, BF16)] * 7 + [act(LANES, BF16), act(LANES, F32), act(D_MODEL, BF16), act(D_MODEL, BF16),
                                           jax.ShapeDtypeStruct((t // MOBA_BLOCK, 1, WIDTH), F32)])
    out_specs = ([pl.BlockSpec((IN_TM, WIDTH), row)] * 7
                 + [pl.BlockSpec((IN_TM, LANES), row), pl.BlockSpec((IN_TM, LANES), row),
                    pl.BlockSpec((IN_TM, D_MODEL), row), pl.BlockSpec((IN_TM, D_MODEL), row),
                    pl.BlockSpec((IN_TM // MOBA_BLOCK, 1, WIDTH), lambda i: (i, 0, 0))])
    return pl.pallas_call(
        _in_proj_kernel,
        grid=(n_tiles,),
        in_specs=[pl.BlockSpec((IN_TM, D_MODEL), row),
                  pl.BlockSpec((1, D_MODEL), const),
                  pl.BlockSpec((D_MODEL, IN_COLS_PADDED), const, pipeline_mode=pl.Buffered(1)),
                  pl.BlockSpec((IN_TM, LANES), lambda i: (i % tiles_per_seq, 0)),
                  pl.BlockSpec((IN_TM, LANES), lambda i: (i % tiles_per_seq, 0))],
        out_specs=out_specs,
        out_shape=out_shape,
        compiler_params=_cparams("parallel"),
        name="in_proj",
    )(x2, g_mix, w_r, cos_t, sin_t)


def _moba_select_kernel(qa_ref, km_ref, selb_ref):
    own = pl.program_id(1)
    n_blocks = km_ref.shape[1]
    km = km_ref[0]
    col_head = lax.broadcasted_iota(I32, (N_HEADS, WIDTH), 1) // HEAD_DIM
    head_mask = col_head == lax.broadcasted_iota(I32, (N_HEADS, WIDTH), 0)
    rows = [jnp.where(head_mask, km[n:n + 1, :], 0.0) for n in range(n_blocks)]
    rows.append(jnp.zeros((LANES - n_blocks * N_HEADS, WIDTH), F32))
    km_t = jnp.concatenate(rows, axis=0).astype(BF16)
    gate = _dot_nt(qa_ref[...], km_t)
    lane = lax.broadcasted_iota(I32, gate.shape, 1)
    past = (lane // N_HEADS) < own
    g = jnp.where(past, gate, NEG)
    rank = jnp.zeros(gate.shape, I32)
    for r in range(1, n_blocks):
        later = pltpu.roll(g, LANES - N_HEADS * r, 1)
        earlier = pltpu.roll(g, N_HEADS * r, 1)
        rank = rank + (later > g).astype(I32) + (earlier >= g).astype(I32)
    chosen = jnp.where(past & (rank < MOBA_TOPK), 1.0, 0.0).astype(BF16)
    src = lax.broadcasted_iota(I32, (LANES, N_PAIRS * LANES), 0)
    dst = lax.broadcasted_iota(I32, (LANES, N_PAIRS * LANES), 1)
    src_n, src_h = src // N_HEADS, src % N_HEADS
    dst_pair, dst_w = dst // LANES, dst % LANES
    expand = ((src_n < n_blocks) & (dst_w < 2 * N_HEADS) & (dst_w % N_HEADS == src_n)
              & (dst_pair * 2 + dst_w // N_HEADS == src_h))
    hit = _dot(chosen, jnp.where(expand, 1.0, 0.0).astype(BF16))
    selb_ref[...] = jnp.where(hit > 0.5, 0.0, NEG)


def _moba_select(qa, kmean, batch, seq):
    n_blocks = seq // MOBA_BLOCK
    assert n_blocks * N_HEADS <= LANES and n_blocks <= N_HEADS
    t = batch * seq
    return pl.pallas_call(
        _moba_select_kernel,
        grid=(batch, n_blocks),
        in_specs=[pl.BlockSpec((MOBA_BLOCK, WIDTH), lambda b, i: (b * n_blocks + i, 0)),
                  pl.BlockSpec((1, n_blocks, WIDTH), lambda b, i: (b, 0, 0))],
        out_specs=pl.BlockSpec((MOBA_BLOCK, N_PAIRS * LANES), lambda b, i: (b * n_blocks + i, 0)),
        out_shape=jax.ShapeDtypeStruct((t, N_PAIRS * LANES), F32),
        compiler_params=_cparams("parallel", "parallel"),
        name="moba_select",
    )(qa, kmean.reshape(batch, n_blocks, WIDTH))


def _stack_heads(q2):
    lane = lax.broadcasted_iota(I32, q2.shape, 1)
    zero = jnp.zeros_like(q2)
    return jnp.concatenate([jnp.where(lane < HEAD_DIM, q2, zero), jnp.where(lane >= HEAD_DIM, q2, zero)], axis=0)


def _unstack_heads(o):
    rows = o.shape[0] // 2
    lane = lax.broadcasted_iota(I32, (rows, LANES), 1)
    return jnp.where(lane < HEAD_DIM, o[:rows], o[rows:])


def _masked_attention(s, v):
    p = jnp.exp(s - jnp.max(s, axis=-1, keepdims=True))
    return _dot(p.astype(BF16), v) / jnp.sum(p, axis=-1, keepdims=True)


def _moba_attn_kernel(q_ref, k_ref, v_ref, selb_ref, o_ref):
    own = pl.program_id(2)
    n_blocks = k_ref.shape[0] // MOBA_BLOCK
    qs = _stack_heads(q_ref[...])
    selb = selb_ref[...]
    q_pos = lax.broadcasted_iota(I32, (2 * MOBA_BLOCK, MOBA_BLOCK), 0) % MOBA_BLOCK
    causal = lax.broadcasted_iota(I32, (2 * MOBA_BLOCK, MOBA_BLOCK), 1) <= q_pos
    for i in range(n_blocks):
        @pl.when(own == i)
        def _(i=i):
            n_keys = (i + 1) * MOBA_BLOCK
            s = _dot_nt(qs, k_ref[0:n_keys, :])
            parts = []
            for j in range(i):
                bias = jnp.concatenate([selb[:, j:j + 1], selb[:, N_HEADS + j:N_HEADS + j + 1]], axis=0)
                parts.append(s[:, j * MOBA_BLOCK:(j + 1) * MOBA_BLOCK] + bias)
            parts.append(jnp.where(causal, s[:, i * MOBA_BLOCK:], NEG))
            o = _masked_attention(jnp.concatenate(parts, axis=1), v_ref[0:n_keys, :])
            o_ref[...] = _unstack_heads(o).astype(o_ref.dtype)


def _moba_attn(qa, ka, va, selb, batch, seq):
    n_blocks = seq // MOBA_BLOCK
    t = batch * seq
    q_map = lambda b, p, i: (b * n_blocks + i, p)
    kv_map = lambda b, p, i: (b, p)
    return pl.pallas_call(
        _moba_attn_kernel,
        grid=(batch, N_PAIRS, n_blocks),
        in_specs=[pl.BlockSpec((MOBA_BLOCK, LANES), q_map),
                  pl.BlockSpec((seq, LANES), kv_map),
                  pl.BlockSpec((seq, LANES), kv_map),
                  pl.BlockSpec((MOBA_BLOCK, LANES), q_map)],
        out_specs=pl.BlockSpec((MOBA_BLOCK, LANES), q_map),
        out_shape=jax.ShapeDtypeStruct((t, WIDTH), BF16),
        compiler_params=_cparams("parallel", "parallel", "arbitrary"),
        name="moba_attn",
    )(qa, ka, va, selb)


def _rope_tables(seq):
    inv = jnp.power(ROPE_THETA, -jnp.arange(0, HEAD_DIM, 2, dtype=F32) / HEAD_DIM)
    ang = jnp.arange(seq, dtype=F32)[:, None] * inv[None, :]
    cos, sin = jnp.cos(ang), jnp.sin(ang)
    reps = LANES // HEAD_DIM
    return jnp.tile(jnp.concatenate([cos, cos], axis=-1), (1, reps)), jnp.tile(jnp.concatenate([-sin, sin], axis=-1), (1, reps))


def _arrange_w_in(w_in):
    sizes = (WIDTH,) * 7 + (HEAD_DIM, N_HEADS, D_MODEL, D_MODEL)
    offs = [0]
    for s in sizes:
        offs.append(offs[-1] + s)
    seg = [w_in[:, offs[i]:offs[i + 1]] for i in range(len(sizes))]
    qa, ka, va, qb, kb, vb, qi, ki, wi, ga, gb = seg
    wi_pad = jnp.pad(wi, ((0, 0), (0, LANES - N_HEADS)))
    w = jnp.concatenate([qa * ATTN_SCALE, ka, va, qb * ATTN_SCALE, kb, vb, qi, ki, ki, wi_pad, ga, gb], axis=1)
    return w.astype(BF16)


DSA_TQ = 256
DSA_ROWS = 64
INT_MIN = -(2 ** 31)


def _count_lanes(hit):
    acc = hit[:, :LANES]
    for j in range(1, hit.shape[1] // LANES):
        acc = acc + hit[:, j * LANES:(j + 1) * LANES]
    return jnp.sum(acc, axis=-1, keepdims=True)


def _dsa_chunk(i, qi_ref, ki_ref, wi_ref, q_ref, k_ref, v_ref, o_ref, score_ref, bias_ref):
    n_keys = (i + 1) * DSA_TQ
    row = lax.broadcasted_iota(I32, (DSA_TQ, DSA_TQ), 0)
    col = lax.broadcasted_iota(I32, (DSA_TQ, DSA_TQ), 1)

    qi = qi_ref[...]
    wi = wi_ref[...]
    ki = ki_ref[0:n_keys, :]
    score = jnp.zeros((DSA_TQ, n_keys), F32)
    for p in range(N_PAIRS):
        logit = _dot_nt(_stack_heads(qi[:, p * LANES:(p + 1) * LANES]), ki)
        score = score + wi[:, 2 * p:2 * p + 1] * jnp.maximum(logit[:DSA_TQ], 0.0)
        score = score + wi[:, 2 * p + 1:2 * p + 2] * jnp.maximum(logit[DSA_TQ:], 0.0)
    diag = jnp.where(col <= row, score[:, i * DSA_TQ:], NEG)
    score = diag if i == 0 else jnp.concatenate([score[:, :i * DSA_TQ], diag], axis=1)
    score_ref[:, 0:n_keys] = score

    def as_float(code):
        return pltpu.bitcast(code ^ ((code >> 31) & 0x7FFFFFFF), F32)

    def count_ge(rg, cand):
        k = score_ref[rg * DSA_ROWS:(rg + 1) * DSA_ROWS, 0:n_keys]
        return _count_lanes(jnp.where(k >= as_float(cand), 1.0, 0.0))

    n_rg = DSA_TQ // DSA_ROWS

    def first_bit(rg):
        cand = jnp.zeros((DSA_ROWS, 1), I32)
        return jnp.where(count_ge(rg, cand) >= DSA_TOPK, cand, INT_MIN)

    def bit_body(ib, ts):
        bit = jnp.left_shift(jnp.int32(1), 30 - ib)
        out = []
        for rg in range(n_rg):
            cand = ts[rg] | bit
            out.append(jnp.where(count_ge(rg, cand) >= DSA_TOPK, cand, ts[rg]))
        return tuple(out)

    ts = lax.fori_loop(0, 31, bit_body, tuple(first_bit(rg) for rg in range(n_rg)))
    thr = as_float(jnp.concatenate(ts, axis=0))

    keys = score_ref[:, 0:n_keys]
    need = float(DSA_TOPK) - _count_lanes(jnp.where(keys > thr, 1.0, 0.0))
    strictly_before = jnp.where(row < col, 1.0, 0.0).astype(BF16)
    ties_seen = jnp.zeros((DSA_TQ, 1), F32)
    for t in range(i + 1):
        k = keys[:, t * DSA_TQ:(t + 1) * DSA_TQ]
        tie = jnp.where(k == thr, 1.0, 0.0)
        ties_before = ties_seen + _dot(tie.astype(BF16), strictly_before)
        chosen = (k > thr) | ((k == thr) & (ties_before < need))
        if t == i:
            chosen = chosen & (col <= row)
        bias_ref[:, t * DSA_TQ:(t + 1) * DSA_TQ] = jnp.where(chosen, 0.0, NEG)
        ties_seen = ties_seen + jnp.sum(tie, axis=-1, keepdims=True)

    bias = bias_ref[:, 0:n_keys]
    bias = jnp.concatenate([bias, bias], axis=0)
    for p in range(N_PAIRS):
        pair = slice(p * LANES, (p + 1) * LANES)
        s = _dot_nt(_stack_heads(q_ref[:, pair]), k_ref[0:n_keys, pair]) + bias
        o_ref[:, pair] = _unstack_heads(_masked_attention(s, v_ref[0:n_keys, pair])).astype(o_ref.dtype)


def _dsa_kernel(qi_ref, ki_ref, wi_ref, q_ref, k_ref, v_ref, o_ref, score_ref, bias_ref):
    for i in range(k_ref.shape[0] // DSA_TQ):
        pl.when(pl.program_id(1) == i)(functools.partial(
            _dsa_chunk, i, qi_ref, ki_ref, wi_ref, q_ref, k_ref, v_ref, o_ref, score_ref, bias_ref))


def _dsa_attn(qi, ki2, wi, qb, kb, vb, batch, seq):
    assert seq % DSA_TQ == 0 and min(DSA_TOPK, seq // 4) == DSA_TOPK
    n_chunks = seq // DSA_TQ
    t = batch * seq
    q_map = lambda b, c: (b * n_chunks + c, 0)
    kv_map = lambda b, c: (b, 0)
    return pl.pallas_call(
        _dsa_kernel,
        grid=(batch, n_chunks),
        in_specs=[pl.BlockSpec((DSA_TQ, WIDTH), q_map),
                  pl.BlockSpec((seq, LANES), kv_map),
                  pl.BlockSpec((DSA_TQ, LANES), q_map),
                  pl.BlockSpec((DSA_TQ, WIDTH), q_map),
                  pl.BlockSpec((seq, WIDTH), kv_map),
                  pl.BlockSpec((seq, WIDTH), kv_map)],
        out_specs=pl.BlockSpec((DSA_TQ, WIDTH), q_map),
        out_shape=jax.ShapeDtypeStruct((t, WIDTH), BF16),
        scratch_shapes=[pltpu.VMEM((DSA_TQ, seq), F32), pltpu.VMEM((DSA_TQ, seq), F32)],
        compiler_params=_cparams("parallel", "arbitrary"),
        name="dsa_attn",
    )(qi, ki2, wi, qb, kb, vb)


MIX_TM = 512
ROUTE_ROWS = 8


def _mix_out_kernel(x_ref, oa_ref, ob_ref, sga_ref, sgb_ref, wa_ref, wb_ref, wo_ref, g_ref, wr_hi_ref, wr_lo_ref, br_ref,
                    x1_ref, xs_ref, route_ref, route_t_ref):
    mixed = (sga_ref[...].astype(F32) * _dot(oa_ref[...], wa_ref[...])
             + sgb_ref[...].astype(F32) * _dot(ob_ref[...], wb_ref[...]))
    x1 = x_ref[...] + _dot(mixed.astype(BF16), wo_ref[...])
    x1_ref[...] = x1
    hn = x1 * lax.rsqrt(jnp.mean(x1 * x1, axis=-1, keepdims=True) + RMS_EPS) * g_ref[...]
    xs_ref[...] = hn

    hi = hn.astype(BF16)
    lo = (hn - hi.astype(F32)).astype(BF16)
    logits = _dot(hi, wr_hi_ref[...]) + _dot(lo, wr_hi_ref[...]) + _dot(hi, wr_lo_ref[...]) + br_ref[...]
    lane = lax.broadcasted_iota(I32, logits.shape, 1)
    far = jnp.int32(LANES)

    def first_lane_of_max(v, valid):
        top = jnp.max(jnp.where(valid, v, NEG), axis=-1, keepdims=True)
        return top, jnp.min(jnp.where(valid & (v == top), lane, far), axis=-1, keepdims=True)

    is_group = lane < N_GROUPS
    g_max, g_sel = first_lane_of_max(logits, is_group)
    g_w = 1.0 / jnp.sum(jnp.where(is_group, jnp.exp(logits - g_max), 0.0), axis=-1, keepdims=True)
    first = N_GROUPS + g_sel * EXPERTS_PER_GROUP
    in_group = (lane >= first) & (lane < first + EXPERTS_PER_GROUP)
    e_max, _ = first_lane_of_max(logits, in_group)
    e_exp = jnp.where(in_group, jnp.exp(logits - e_max), 0.0)
    prob = e_exp / jnp.sum(e_exp, axis=-1, keepdims=True)
    p1, i1 = first_lane_of_max(prob, in_group)
    p2, i2 = first_lane_of_max(prob, in_group & (lane != i1))
    denom = p1 + p2
    record = jnp.where(lane == 0, (i1 - N_GROUPS).astype(F32),
                       jnp.where(lane == 1, (i2 - N_GROUPS).astype(F32),
                                 jnp.where(lane == 2, g_w * p1 / denom,
                                           jnp.where(lane == 3, g_w * p2 / denom, 0.0))))
    route_ref[...] = record
    route_t_ref[...] = record.T[:ROUTE_ROWS, :]


def _mix_out(x2, oa, ob, sga, sgb, wa, wb, wo, g_ffn, wr_hi, wr_lo, br):
    t = x2.shape[0]
    row = lambda i: (i, 0)
    const = lambda i: (0, 0)
    once = dict(pipeline_mode=pl.Buffered(1))
    return pl.pallas_call(
        _mix_out_kernel,
        grid=(t // MIX_TM,),
        in_specs=[pl.BlockSpec((MIX_TM, D_MODEL), row),
                  pl.BlockSpec((MIX_TM, WIDTH), row), pl.BlockSpec((MIX_TM, WIDTH), row),
                  pl.BlockSpec((MIX_TM, D_MODEL), row), pl.BlockSpec((MIX_TM, D_MODEL), row),
                  pl.BlockSpec((WIDTH, D_MODEL), const, **once), pl.BlockSpec((WIDTH, D_MODEL), const, **once),
                  pl.BlockSpec((D_MODEL, D_MODEL), const, **once), pl.BlockSpec((1, D_MODEL), const),
                  pl.BlockSpec((D_MODEL, LANES), const, **once), pl.BlockSpec((D_MODEL, LANES), const, **once),
                  pl.BlockSpec((1, LANES), const)],
        out_specs=[pl.BlockSpec((MIX_TM, D_MODEL), row), pl.BlockSpec((MIX_TM, D_MODEL), row),
                   pl.BlockSpec((MIX_TM, LANES), row), pl.BlockSpec((ROUTE_ROWS, MIX_TM), lambda i: (0, i))],
        out_shape=[jax.ShapeDtypeStruct((t, D_MODEL), F32), jax.ShapeDtypeStruct((t, D_MODEL), F32),
                   jax.ShapeDtypeStruct((t, LANES), F32), jax.ShapeDtypeStruct((ROUTE_ROWS, t), F32)],
        compiler_params=_cparams("parallel"),
        name="mix_out",
    )(x2, oa, ob, sga, sgb, wa, wb, wo, g_ffn, wr_hi, wr_lo, br)


PLAN_TM = 512


def _moe_plan_kernel(route_t_ref, dest_ref, counts_ref, count_ref, start_ref):
    phase = pl.program_id(0)
    step = pl.program_id(1)
    expert = lax.broadcasted_iota(I32, (LANES, PLAN_TM), 0)
    e1 = route_t_ref[0:1, :].astype(I32)
    e2 = route_t_ref[1:2, :].astype(I32)
    hot1 = expert == e1
    hot2 = expert == e2
    hot = jnp.where(hot1 | hot2, 1.0, 0.0)

    @pl.when((phase == 0) & (step == 0))
    def _():
        count_ref[...] = jnp.zeros_like(count_ref)

    @pl.when(phase == 0)
    def _():
        count_ref[...] += jnp.sum(hot, axis=-1, keepdims=True)
        dest_ref[...] = jnp.zeros_like(dest_ref)

    @pl.when((phase == 1) & (step == 0))
    def _():
        counts = jnp.broadcast_to(count_ref[...], (LANES, LANES))
        counts_ref[...] = counts
        padded = jnp.ceil(counts / DISPATCH_BLOCK) * DISPATCH_BLOCK
        sub = lax.broadcasted_iota(I32, (LANES, LANES), 0)
        ends = padded
        shift = 1
        while shift < LANES:
            ends = ends + jnp.where(sub >= shift, pltpu.roll(ends, shift, 0), 0.0)
            shift *= 2
        start_ref[...] = (ends - padded)[:, 0:1]
        count_ref[...] = jnp.zeros_like(count_ref)

    @pl.when(phase == 1)
    def _():
        tok_r = lax.broadcasted_iota(I32, (PLAN_TM, PLAN_TM), 0)
        tok_c = lax.broadcasted_iota(I32, (PLAN_TM, PLAN_TM), 1)
        earlier = jnp.where(tok_r < tok_c, 1.0, 0.0).astype(BF16)
        slot = start_ref[...] + count_ref[...] + _dot(hot.astype(BF16), earlier)
        d1 = jnp.sum(jnp.where(hot1, slot, 0.0), axis=0, keepdims=True)
        d2 = jnp.sum(jnp.where(hot2, slot, 0.0), axis=0, keepdims=True)
        sub = lax.broadcasted_iota(I32, (ROUTE_ROWS, PLAN_TM), 0)
        dest_ref[...] = jnp.where(sub == 0, d1, jnp.where(sub == 1, d2, 0.0)).astype(I32)
        count_ref[...] += jnp.sum(hot, axis=-1, keepdims=True)


def _moe_plan(route_t):
    t = route_t.shape[1]
    return pl.pallas_call(
        _moe_plan_kernel,
        grid=(2, t // PLAN_TM),
        in_specs=[pl.BlockSpec((ROUTE_ROWS, PLAN_TM), lambda ph, i: (0, i))],
        out_specs=[pl.BlockSpec((ROUTE_ROWS, PLAN_TM), lambda ph, i: (0, i * ph)),
                   pl.BlockSpec((LANES, LANES), lambda ph, i: (0, 0))],
        out_shape=[jax.ShapeDtypeStruct((ROUTE_ROWS, t), I32), jax.ShapeDtypeStruct((LANES, LANES), F32)],
        scratch_shapes=[pltpu.VMEM((LANES, 1), F32), pltpu.VMEM((LANES, 1), F32)],
        compiler_params=_cparams("arbitrary", "arbitrary"),
        name="moe_plan",
    )(route_t)


DISP_TM = 1024


def _moe_dispatch_kernel(d1_ref, d2_ref, xs_ref, zeros_ref, out_ref, sem):
    del zeros_ref

    def row_copy(r, dest):
        return pltpu.make_async_copy(xs_ref.at[pl.ds(r, 1), :], out_ref.at[pl.ds(dest, 1), :], sem)

    def issue(r, carry):
        row_copy(r, d1_ref[r]).start()
        row_copy(r, d2_ref[r]).start()
        return carry
    lax.fori_loop(0, DISP_TM, issue, 0)

    for _ in range(2):
        pltpu.make_async_copy(xs_ref, out_ref.at[pl.ds(0, DISP_TM), :], sem).wait()


def _moe_dispatch(d1, d2, xs, n_rows):
    t = xs.shape[0]
    smem = lambda: pl.BlockSpec((DISP_TM,), lambda i: (i,), memory_space=pltpu.SMEM)
    return pl.pallas_call(
        _moe_dispatch_kernel,
        grid=(t // DISP_TM,),
        in_specs=[smem(), smem(), pl.BlockSpec((DISP_TM, D_MODEL), lambda i: (i, 0)),
                  pl.BlockSpec(memory_space=pl.ANY)],
        out_specs=pl.BlockSpec(memory_space=pl.ANY),
        out_shape=jax.ShapeDtypeStruct((n_rows, D_MODEL), F32),
        scratch_shapes=[pltpu.SemaphoreType.DMA(())],
        input_output_aliases={3: 0},
        compiler_params=_cparams("arbitrary"),
        name="moe_dispatch",
    )(d1, d2, xs, jnp.zeros((n_rows, D_MODEL), F32))


def _moe_expert_kernel(block_expert_ref, n_used_ref, xs_ref, w1_ref, w3_ref, w2_ref, ys_ref):
    del block_expert_ref
    live = pl.program_id(0) < n_used_ref[0]

    @pl.when(live)
    def _():
        x = xs_ref[...].astype(BF16)
        h1 = _dot(x, w1_ref[0])
        h3 = _dot(x, w3_ref[0])
        hid = h1 / (1.0 + jnp.exp(-h1)) * h3
        ys_ref[...] = _dot(hid.astype(BF16), w2_ref[0])

    @pl.when(jnp.logical_not(live))
    def _():
        ys_ref[...] = jnp.zeros_like(ys_ref)


def _moe_experts(block_expert, n_used, xs_sorted, w1, w3, w2):
    n_rows = xs_sorted.shape[0]
    grid_spec = pltpu.PrefetchScalarGridSpec(
        num_scalar_prefetch=2,
        grid=(n_rows // DISPATCH_BLOCK,),
        in_specs=[pl.BlockSpec((DISPATCH_BLOCK, D_MODEL), lambda j, be, nu: (j, 0)),
                  pl.BlockSpec((1, D_MODEL, EXPERT_FF), lambda j, be, nu: (be[j], 0, 0)),
                  pl.BlockSpec((1, D_MODEL, EXPERT_FF), lambda j, be, nu: (be[j], 0, 0)),
                  pl.BlockSpec((1, EXPERT_FF, D_MODEL), lambda j, be, nu: (be[j], 0, 0))],
        out_specs=pl.BlockSpec((DISPATCH_BLOCK, D_MODEL), lambda j, be, nu: (j, 0)))
    return pl.pallas_call(
        _moe_expert_kernel,
        grid_spec=grid_spec,
        out_shape=jax.ShapeDtypeStruct((n_rows, D_MODEL), F32),
        compiler_params=_cparams("arbitrary"),
        name="moe_experts",
    )(block_expert, n_used, xs_sorted, w1, w3, w2)


COMB_TM = 512


def _moe_combine_kernel(d1_ref, d2_ref, x1_ref, route_ref, g_ref, ys_ref, out_ref, y1_ref, y2_ref, sem):
    def row_copy(src, r, buf):
        return pltpu.make_async_copy(ys_ref.at[pl.ds(src, 1), :], buf.at[pl.ds(r, 1), :], sem)

    def issue(r, carry):
        row_copy(d1_ref[r], r, y1_ref).start()
        row_copy(d2_ref[r], r, y2_ref).start()
        return carry
    lax.fori_loop(0, COMB_TM, issue, 0)

    for buf in (y1_ref, y2_ref):
        pltpu.make_async_copy(ys_ref.at[pl.ds(0, COMB_TM), :], buf, sem).wait()

    route = route_ref[...]
    x2 = x1_ref[...] + (route[:, 2:3] * y1_ref[...] + route[:, 3:4] * y2_ref[...])
    out_ref[...] = x2 * lax.rsqrt(jnp.mean(x2 * x2, axis=-1, keepdims=True) + RMS_EPS) * g_ref[...]


def _moe_combine(d1, d2, x1, route, g_final, ys):
    t = x1.shape[0]
    smem = lambda: pl.BlockSpec((COMB_TM,), lambda i: (i,), memory_space=pltpu.SMEM)
    row = lambda i: (i, 0)
    return pl.pallas_call(
        _moe_combine_kernel,
        grid=(t // COMB_TM,),
        in_specs=[smem(), smem(), pl.BlockSpec((COMB_TM, D_MODEL), row), pl.BlockSpec((COMB_TM, LANES), row),
                  pl.BlockSpec((1, D_MODEL), lambda i: (0, 0)), pl.BlockSpec(memory_space=pl.ANY)],
        out_specs=pl.BlockSpec((COMB_TM, D_MODEL), row),
        out_shape=jax.ShapeDtypeStruct((t, D_MODEL), F32),
        scratch_shapes=[pltpu.VMEM((COMB_TM, D_MODEL), F32), pltpu.VMEM((COMB_TM, D_MODEL), F32),
                        pltpu.SemaphoreType.DMA(())],
        compiler_params=_cparams("arbitrary"),
        name="moe_combine",
    )(d1, d2, x1, route, g_final, ys)


def _hier_moe_tail(x1, xs, route, route_t, w1, w3, w2, g_final):
    t = x1.shape[0]
    dest, counts = _moe_plan(route_t)
    counts = counts[:N_EXPERTS, 0].astype(I32)
    padded = (counts + DISPATCH_BLOCK - 1) // DISPATCH_BLOCK * DISPATCH_BLOCK
    ends = jnp.cumsum(padded)
    n_blocks = (t * 2) // DISPATCH_BLOCK + N_EXPERTS
    block_start = jnp.arange(n_blocks, dtype=I32) * DISPATCH_BLOCK
    block_expert = jnp.minimum(jnp.sum((ends[None, :] <= block_start[:, None]).astype(I32), axis=1), N_EXPERTS - 1)
    n_used = (ends[-1:] // DISPATCH_BLOCK).astype(I32)
    d1, d2 = dest[0], dest[1]
    xs_sorted = _moe_dispatch(d1, d2, xs, n_blocks * DISPATCH_BLOCK)
    ys = _moe_experts(block_expert, n_used, xs_sorted, w1, w3, w2)
    return _moe_combine(d1, d2, x1, route, g_final, ys)


def kernel(x, g_mix, w_in, w_proj_a, w_proj_b, w_out, g_ffn, w_group, b_group, w_expert, b_expert, w1, w3, w2, g_final):
    batch, seq, d = x.shape
    assert d == D_MODEL and g_mix.shape[0] == 1, "one layer of width D_MODEL"
    x2 = x.reshape(batch * seq, d)
    cos_t, sin_t = _rope_tables(seq)
    qa, ka, va, qb, kb, vb, qi, ki2, wi, sga, sgb, kmean = _in_proj(
        x2, g_mix[0][None, :], _arrange_w_in(w_in[0]), cos_t, sin_t, seq)
    selb = _moba_select(qa, kmean, batch, seq)
    oa = _moba_attn(qa, ka, va, selb, batch, seq)
    ob = _dsa_attn(qi, ki2, wi, qb, kb, vb, batch, seq)
    wr_hi, wr_lo, br = _router_params(w_group[0], b_group[0], w_expert[0], b_expert[0])
    x1, xs, route, route_t = _mix_out(
        x2, oa, ob, sga, sgb, w_proj_a[0].astype(BF16), w_proj_b[0].astype(BF16), w_out[0].astype(BF16),
        g_ffn[0][None, :], wr_hi, wr_lo, br)
    out = _hier_moe_tail(x1, xs, route, route_t, w1[0].astype(BF16), w3[0].astype(BF16), w2[0].astype(BF16),
                         g_final[None, :])
    return out.reshape(batch, seq, d)


def _router_params(w_group, b_group, w_expert, b_expert):
    w = jnp.pad(jnp.concatenate([w_group, w_expert], axis=1), ((0, 0), (0, LANES - N_GROUPS - N_EXPERTS)))
    b = jnp.pad(jnp.concatenate([b_group, b_expert]), (0, LANES - N_GROUPS - N_EXPERTS))[None, :]
    w_hi = w.astype(BF16)
    w_lo = (w - w_hi.astype(F32)).astype(BF16)
    return w_hi, w_lo, b
```

```python
import functools

import jax
import jax.numpy as jnp
from jax import lax
from jax.experimental import pallas as pl
from jax.experimental.pallas import tpu as pltpu

F32 = jnp.float32
BF16 = jnp.bfloat16
I32 = jnp.int32

D_MODEL = 1024
HEAD_DIM = 64
N_HEADS = 8
WIDTH = N_HEADS * HEAD_DIM
N_PAIRS = N_HEADS // 2
MOBA_BLOCK = 256
MOBA_TOPK = 3
DSA_TOPK = 256
IDX_SCALE = float(WIDTH) ** -0.5
ATTN_SCALE = float(HEAD_DIM) ** -0.5
N_GROUPS = 4
EXPERTS_PER_GROUP = 8
N_EXPERTS = N_GROUPS * EXPERTS_PER_GROUP
EXPERT_FF = 512
DISPATCH_BLOCK = 256
ROPE_THETA = 10000.0
RMS_EPS = 1e-6
NEG = -1e30

LANES = 128
VMEM_LIMIT = 56 * 1024 * 1024

C_QA, C_KA, C_VA, C_QB, C_KB, C_VB, C_QI = (i * WIDTH for i in range(7))
C_KI = 7 * WIDTH
C_WI = C_KI + LANES
C_GA = C_WI + WIDTH
C_GB = C_GA + D_MODEL
IN_COLS_PADDED = C_GB + D_MODEL


def _cparams(*semantics):
    return pltpu.CompilerParams(dimension_semantics=semantics, vmem_limit_bytes=VMEM_LIMIT)


def _dot_nt(a, b):
    return lax.dot_general(a, b, (((1,), (1,)), ((), ())), preferred_element_type=F32)


def _dot(a, b):
    return jnp.dot(a, b, preferred_element_type=F32)


IN_TM = 512


def _in_proj_kernel(x_ref, g_ref, w_ref, cos_ref, sin_ref,
                    qa_ref, ka_ref, va_ref, qb_ref, kb_ref, vb_ref, qi_ref, ki_ref, wi_ref,
                    sga_ref, sgb_ref, kmean_ref):
    x = x_ref[...]
    h = x * lax.rsqrt(jnp.mean(x * x, axis=-1, keepdims=True) + RMS_EPS) * g_ref[...]
    hb = h.astype(BF16)
    cos = cos_ref[...]
    sin = sin_ref[...]
    upper_half = (lax.broadcasted_iota(I32, (IN_TM, LANES), 1) & (HEAD_DIM // 2)) != 0

    def rope(v):
        partner = jnp.where(upper_half, pltpu.roll(v, HEAD_DIM // 2, 1), pltpu.roll(v, LANES - HEAD_DIM // 2, 1))
        return v * cos + partner * sin

    def proj(c0, width):
        return _dot(hb, w_ref[:, c0:c0 + width])

    def store_heads(ref, c0, rotary):
        r = proj(c0, WIDTH)
        for j in range(WIDTH // LANES):
            v = r[:, j * LANES:(j + 1) * LANES]
            ref[:, j * LANES:(j + 1) * LANES] = (rope(v) if rotary else v).astype(ref.dtype)

    store_heads(qa_ref, C_QA, True)
    store_heads(va_ref, C_VA, False)
    store_heads(qb_ref, C_QB, True)
    store_heads(kb_ref, C_KB, True)
    store_heads(vb_ref, C_VB, False)
    store_heads(qi_ref, C_QI, True)

    r = proj(C_KA, WIDTH)
    for j in range(WIDTH // LANES):
        v = rope(r[:, j * LANES:(j + 1) * LANES])
        ka_ref[:, j * LANES:(j + 1) * LANES] = v.astype(BF16)
        for blk in range(IN_TM // MOBA_BLOCK):
            kmean_ref[blk, :, j * LANES:(j + 1) * LANES] = jnp.mean(
                v[blk * MOBA_BLOCK:(blk + 1) * MOBA_BLOCK], axis=0, keepdims=True)

    ki_ref[...] = rope(proj(C_KI, LANES)).astype(BF16)
    wi_ref[...] = proj(C_WI, WIDTH) * IDX_SCALE
    sga_ref[...] = (1.0 / (1.0 + jnp.exp(-proj(C_GA, D_MODEL)))).astype(BF16)
    sgb_ref[...] = (1.0 / (1.0 + jnp.exp(-proj(C_GB, D_MODEL)))).astype(BF16)


def _in_proj(x2, g_mix, w_r, cos_t, sin_t, seq):
    t = x2.shape[0]
    n_tiles = t // IN_TM
    tiles_per_seq = seq // IN_TM
    row = lambda i: (i, 0)
    const = lambda i: (0, 0)
    act = lambda w, dt: jax.ShapeDtypeStruct((t, w), dt)
    out_shape = ([act(WIDTH, BF16)] * 7 + [act(LANES, BF16), act(WIDTH, F32), act(D_MODEL, BF16), act(D_MODEL, BF16),
                                           jax.ShapeDtypeStruct((t // MOBA_BLOCK, 1, WIDTH), F32)])
    out_specs = ([pl.BlockSpec((IN_TM, WIDTH), row)] * 7
                 + [pl.BlockSpec((IN_TM, LANES), row), pl.BlockSpec((IN_TM, WIDTH), row),
                    pl.BlockSpec((IN_TM, D_MODEL), row), pl.BlockSpec((IN_TM, D_MODEL), row),
                    pl.BlockSpec((IN_TM // MOBA_BLOCK, 1, WIDTH), lambda i: (i, 0, 0))])
    return pl.pallas_call(
        _in_proj_kernel,
        grid=(n_tiles,),
        in_specs=[pl.BlockSpec((IN_TM, D_MODEL), row),
                  pl.BlockSpec((1, D_MODEL), const),
                  pl.BlockSpec((D_MODEL, IN_COLS_PADDED), const, pipeline_mode=pl.Buffered(1)),
                  pl.BlockSpec((IN_TM, LANES), lambda i: (i % tiles_per_seq, 0)),
                  pl.BlockSpec((IN_TM, LANES), lambda i: (i % tiles_per_seq, 0))],
        out_specs=out_specs,
        out_shape=out_shape,
        compiler_params=_cparams("parallel"),
        name="in_proj",
    )(x2, g_mix, w_r, cos_t, sin_t)


def _moba_select_kernel(qa_ref, km_ref, selb_ref):
    own = pl.program_id(1)
    n_blocks = km_ref.shape[1]
    km = km_ref[0]
    col_head = lax.broadcasted_iota(I32, (N_HEADS, WIDTH), 1) // HEAD_DIM
    head_mask = col_head == lax.broadcasted_iota(I32, (N_HEADS, WIDTH), 0)
    rows = [jnp.where(head_mask, km[n:n + 1, :], 0.0) for n in range(n_blocks)]
    rows.append(jnp.zeros((LANES - n_blocks * N_HEADS, WIDTH), F32))
    km_t = jnp.concatenate(rows, axis=0).astype(BF16)
    gate = _dot_nt(qa_ref[...], km_t)
    lane = lax.broadcasted_iota(I32, gate.shape, 1)
    past = (lane // N_HEADS) < own
    g = jnp.where(past, gate, NEG)
    rank = jnp.zeros(gate.shape, I32)
    for r in range(1, n_blocks):
        later = pltpu.roll(g, LANES - N_HEADS * r, 1)
        earlier = pltpu.roll(g, N_HEADS * r, 1)
        rank = rank + (later > g).astype(I32) + (earlier >= g).astype(I32)
    chosen = jnp.where(past & (rank < MOBA_TOPK), 1.0, 0.0).astype(BF16)
    src = lax.broadcasted_iota(I32, (LANES, N_PAIRS * LANES), 0)
    dst = lax.broadcasted_iota(I32, (LANES, N_PAIRS * LANES), 1)
    src_n, src_h = src // N_HEADS, src % N_HEADS
    dst_pair, dst_w = dst // LANES, dst % LANES
    expand = ((src_n < n_blocks) & (dst_w < 2 * N_HEADS) & (dst_w % N_HEADS == src_n)
              & (dst_pair * 2 + dst_w // N_HEADS == src_h))
    hit = _dot(chosen, jnp.where(expand, 1.0, 0.0).astype(BF16))
    selb_ref[...] = jnp.where(hit > 0.5, 0.0, NEG)


def _moba_select(qa, kmean, batch, seq):
    n_blocks = seq // MOBA_BLOCK
    assert n_blocks * N_HEADS <= LANES and n_blocks <= N_HEADS
    t = batch * seq
    return pl.pallas_call(
        _moba_select_kernel,
        grid=(batch, n_blocks),
        in_specs=[pl.BlockSpec((MOBA_BLOCK, WIDTH), lambda b, i: (b * n_blocks + i, 0)),
                  pl.BlockSpec((1, n_blocks, WIDTH), lambda b, i: (b, 0, 0))],
        out_specs=pl.BlockSpec((MOBA_BLOCK, N_PAIRS * LANES), lambda b, i: (b * n_blocks + i, 0)),
        out_shape=jax.ShapeDtypeStruct((t, N_PAIRS * LANES), F32),
        compiler_params=_cparams("parallel", "parallel"),
        name="moba_select",
    )(qa, kmean.reshape(batch, n_blocks, WIDTH))


def _stack_heads(q2):
    lane = lax.broadcasted_iota(I32, q2.shape, 1)
    zero = jnp.zeros_like(q2)
    return jnp.concatenate([jnp.where(lane < HEAD_DIM, q2, zero), jnp.where(lane >= HEAD_DIM, q2, zero)], axis=0)


def _unstack_heads(o):
    rows = o.shape[0] // 2
    lane = lax.broadcasted_iota(I32, (rows, LANES), 1)
    return jnp.where(lane < HEAD_DIM, o[:rows], o[rows:])


def _masked_attention(s, v):
    p = jnp.exp(s - jnp.max(s, axis=-1, keepdims=True))
    return _dot(p.astype(BF16), v) / jnp.sum(p, axis=-1, keepdims=True)


def _moba_attn_kernel(q_ref, k_ref, v_ref, selb_ref, o_ref):
    own = pl.program_id(2)
    n_blocks = k_ref.shape[0] // MOBA_BLOCK
    qs = _stack_heads(q_ref[...])
    selb = selb_ref[...]
    q_pos = lax.broadcasted_iota(I32, (2 * MOBA_BLOCK, MOBA_BLOCK), 0) % MOBA_BLOCK
    causal = lax.broadcasted_iota(I32, (2 * MOBA_BLOCK, MOBA_BLOCK), 1) <= q_pos
    for i in range(n_blocks):
        @pl.when(own == i)
        def _(i=i):
            n_keys = (i + 1) * MOBA_BLOCK
            s = _dot_nt(qs, k_ref[0:n_keys, :])
            parts = []
            for j in range(i):
                bias = jnp.concatenate([selb[:, j:j + 1], selb[:, N_HEADS + j:N_HEADS + j + 1]], axis=0)
                parts.append(s[:, j * MOBA_BLOCK:(j + 1) * MOBA_BLOCK] + bias)
            parts.append(jnp.where(causal, s[:, i * MOBA_BLOCK:], NEG))
            o = _masked_attention(jnp.concatenate(parts, axis=1), v_ref[0:n_keys, :])
            o_ref[...] = _unstack_heads(o).astype(o_ref.dtype)


def _moba_attn(qa, ka, va, selb, batch, seq):
    n_blocks = seq // MOBA_BLOCK
    t = batch * seq
    q_map = lambda b, p, i: (b * n_blocks + i, p)
    kv_map = lambda b, p, i: (b, p)
    return pl.pallas_call(
        _moba_attn_kernel,
        grid=(batch, N_PAIRS, n_blocks),
        in_specs=[pl.BlockSpec((MOBA_BLOCK, LANES), q_map),
                  pl.BlockSpec((seq, LANES), kv_map),
                  pl.BlockSpec((seq, LANES), kv_map),
                  pl.BlockSpec((MOBA_BLOCK, LANES), q_map)],
        out_specs=pl.BlockSpec((MOBA_BLOCK, LANES), q_map),
        out_shape=jax.ShapeDtypeStruct((t, WIDTH), BF16),
        compiler_params=_cparams("parallel", "parallel", "arbitrary"),
        name="moba_attn",
    )(qa, ka, va, selb)


def _rope_tables(seq):
    inv = jnp.power(ROPE_THETA, -jnp.arange(0, HEAD_DIM, 2, dtype=F32) / HEAD_DIM)
    ang = jnp.arange(seq, dtype=F32)[:, None] * inv[None, :]
    cos, sin = jnp.cos(ang), jnp.sin(ang)
    reps = LANES // HEAD_DIM
    return jnp.tile(jnp.concatenate([cos, cos], axis=-1), (1, reps)), jnp.tile(jnp.concatenate([-sin, sin], axis=-1), (1, reps))


def _arrange_w_in(w_in):
    sizes = (WIDTH,) * 7 + (HEAD_DIM, N_HEADS, D_MODEL, D_MODEL)
    offs = [0]
    for s in sizes:
        offs.append(offs[-1] + s)
    seg = [w_in[:, offs[i]:offs[i + 1]] for i in range(len(sizes))]
    qa, ka, va, qb, kb, vb, qi, ki, wi, ga, gb = seg
    wi_pairs = jnp.pad(wi.reshape(-1, N_PAIRS, 2), ((0, 0), (0, 0), (0, LANES - 2))).reshape(-1, WIDTH)
    w = jnp.concatenate([qa * ATTN_SCALE, ka, va, qb * ATTN_SCALE, kb, vb, qi, ki, ki, wi_pairs, ga, gb], axis=1)
    return w.astype(BF16)


DSA_TQ = 256
DSA_ROWS = 64
INT_MIN = -(2 ** 31)


def _count_lanes(hit):
    acc = hit[:, :LANES]
    for j in range(1, hit.shape[1] // LANES):
        acc = acc + hit[:, j * LANES:(j + 1) * LANES]
    return jnp.sum(acc, axis=-1, keepdims=True)


def _dsa_score_pair(i, qi_ref, ki_ref, wi_ref, score_ref):
    n_keys = (i + 1) * DSA_TQ
    logit = _dot_nt(_stack_heads(qi_ref[...]), ki_ref[0:n_keys, :])
    wi = wi_ref[...]
    part = wi[:, 0:1] * jnp.maximum(logit[:DSA_TQ], 0.0) + wi[:, 1:2] * jnp.maximum(logit[DSA_TQ:], 0.0)
    pair = pl.program_id(2)

    @pl.when(pair == 0)
    def _():
        score_ref[:, 0:n_keys] = part

    @pl.when(pair > 0)
    def _():
        score_ref[:, 0:n_keys] += part


def _dsa_select_chunk(i, score_ref, bias_ref):
    n_keys = (i + 1) * DSA_TQ
    row = lax.broadcasted_iota(I32, (DSA_TQ, DSA_TQ), 0)
    col = lax.broadcasted_iota(I32, (DSA_TQ, DSA_TQ), 1)
    score_ref[:, i * DSA_TQ:n_keys] = jnp.where(col <= row, score_ref[:, i * DSA_TQ:n_keys], NEG)

    def as_float(code):
        return pltpu.bitcast(code ^ ((code >> 31) & 0x7FFFFFFF), F32)

    def count_ge(rg, cand):
        k = score_ref[rg * DSA_ROWS:(rg + 1) * DSA_ROWS, 0:n_keys]
        return _count_lanes(jnp.where(k >= as_float(cand), 1.0, 0.0))

    n_rg = DSA_TQ // DSA_ROWS

    def first_bit(rg):
        cand = jnp.zeros((DSA_ROWS, 1), I32)
        return jnp.where(count_ge(rg, cand) >= DSA_TOPK, cand, INT_MIN)

    def bit_body(ib, ts):
        bit = jnp.left_shift(jnp.int32(1), 30 - ib)
        out = []
        for rg in range(n_rg):
            cand = ts[rg] | bit
            out.append(jnp.where(count_ge(rg, cand) >= DSA_TOPK, cand, ts[rg]))
        return tuple(out)

    ts = lax.fori_loop(0, 31, bit_body, tuple(first_bit(rg) for rg in range(n_rg)))
    thr = as_float(jnp.concatenate(ts, axis=0))

    keys = score_ref[:, 0:n_keys]
    need = float(DSA_TOPK) - _count_lanes(jnp.where(keys > thr, 1.0, 0.0))
    strictly_before = jnp.where(row < col, 1.0, 0.0).astype(BF16)
    ties_seen = jnp.zeros((DSA_TQ, 1), F32)
    for t in range(i + 1):
        k = keys[:, t * DSA_TQ:(t + 1) * DSA_TQ]
        tie = jnp.where(k == thr, 1.0, 0.0)
        ties_before = ties_seen + _dot(tie.astype(BF16), strictly_before)
        chosen = (k > thr) | ((k == thr) & (ties_before < need))
        if t == i:
            chosen = chosen & (col <= row)
        bias_ref[:, t * DSA_TQ:(t + 1) * DSA_TQ] = jnp.where(chosen, 0.0, NEG).astype(bias_ref.dtype)
        ties_seen = ties_seen + jnp.sum(tie, axis=-1, keepdims=True)
    if n_keys < bias_ref.shape[1]:
        bias_ref[:, n_keys:] = jnp.full((DSA_TQ, bias_ref.shape[1] - n_keys), NEG, bias_ref.dtype)


def _dsa_select_kernel(qi_ref, ki_ref, wi_ref, bias_ref, score_ref):
    for i in range(ki_ref.shape[0] // DSA_TQ):
        @pl.when(pl.program_id(1) == i)
        def _(i=i):
            _dsa_score_pair(i, qi_ref, ki_ref, wi_ref, score_ref)
            pl.when(pl.program_id(2) == N_PAIRS - 1)(functools.partial(_dsa_select_chunk, i, score_ref, bias_ref))


def _dsa_select(qi, ki2, wi, batch, seq):
    assert seq % DSA_TQ == 0 and min(DSA_TOPK, seq // 4) == DSA_TOPK
    n_chunks = seq // DSA_TQ
    q_map = lambda b, c, p: (b * n_chunks + c, p)
    return pl.pallas_call(
        _dsa_select_kernel,
        grid=(batch, n_chunks, N_PAIRS),
        in_specs=[pl.BlockSpec((DSA_TQ, LANES), q_map),
                  pl.BlockSpec((seq, LANES), lambda b, c, p: (b, 0)),
                  pl.BlockSpec((DSA_TQ, LANES), q_map)],
        out_specs=pl.BlockSpec((DSA_TQ, seq), lambda b, c, p: (b * n_chunks + c, 0)),
        out_shape=jax.ShapeDtypeStruct((batch * seq, seq), BF16),
        scratch_shapes=[pltpu.VMEM((DSA_TQ, seq), F32)],
        compiler_params=_cparams("parallel", "arbitrary", "arbitrary"),
        name="dsa_select",
    )(qi, ki2, wi)


def _dsa_attend_kernel(q_ref, k_ref, v_ref, bias_ref, o_ref):
    qs = _stack_heads(q_ref[...])
    for i in range(k_ref.shape[0] // DSA_TQ):
        @pl.when(pl.program_id(1) == i)
        def _(i=i):
            n_keys = (i + 1) * DSA_TQ
            bias = bias_ref[:, 0:n_keys].astype(F32)
            s = _dot_nt(qs, k_ref[0:n_keys, :]) + jnp.concatenate([bias, bias], axis=0)
            o_ref[...] = _unstack_heads(_masked_attention(s, v_ref[0:n_keys, :])).astype(o_ref.dtype)


def _dsa_attend(qb, kb, vb, bias, batch, seq):
    n_chunks = seq // DSA_TQ
    q_map = lambda b, c, p: (b * n_chunks + c, p)
    kv_map = lambda b, c, p: (b, p)
    return pl.pallas_call(
        _dsa_attend_kernel,
        grid=(batch, n_chunks, N_PAIRS),
        in_specs=[pl.BlockSpec((DSA_TQ, LANES), q_map),
                  pl.BlockSpec((seq, LANES), kv_map),
                  pl.BlockSpec((seq, LANES), kv_map),
                  pl.BlockSpec((DSA_TQ, seq), lambda b, c, p: (b * n_chunks + c, 0))],
        out_specs=pl.BlockSpec((DSA_TQ, LANES), q_map),
        out_shape=jax.ShapeDtypeStruct((batch * seq, WIDTH), BF16),
        compiler_params=_cparams("parallel", "parallel", "arbitrary"),
        name="dsa_attend",
    )(qb, kb, vb, bias)


MIX_TM = 512
ROUTE_ROWS = 8


def _mix_out_kernel(x_ref, oa_ref, ob_ref, sga_ref, sgb_ref, wa_ref, wb_ref, wo_ref, g_ref, wr_hi_ref, wr_lo_ref, br_ref,
                    x1_ref, xs_ref, route_ref, route_t_ref):
    mixed = (sga_ref[...].astype(F32) * _dot(oa_ref[...], wa_ref[...])
             + sgb_ref[...].astype(F32) * _dot(ob_ref[...], wb_ref[...]))
    x1 = x_ref[...] + _dot(mixed.astype(BF16), wo_ref[...])
    x1_ref[...] = x1
    hn = x1 * lax.rsqrt(jnp.mean(x1 * x1, axis=-1, keepdims=True) + RMS_EPS) * g_ref[...]
    xs_ref[...] = hn

    hi = hn.astype(BF16)
    lo = (hn - hi.astype(F32)).astype(BF16)
    logits = _dot(hi, wr_hi_ref[...]) + _dot(lo, wr_hi_ref[...]) + _dot(hi, wr_lo_ref[...]) + br_ref[...]
    lane = lax.broadcasted_iota(I32, logits.shape, 1)
    far = jnp.int32(LANES)

    def first_lane_of_max(v, valid):
        top = jnp.max(jnp.where(valid, v, NEG), axis=-1, keepdims=True)
        return top, jnp.min(jnp.where(valid & (v == top), lane, far), axis=-1, keepdims=True)

    is_group = lane < N_GROUPS
    g_max, g_sel = first_lane_of_max(logits, is_group)
    g_w = 1.0 / jnp.sum(jnp.where(is_group, jnp.exp(logits - g_max), 0.0), axis=-1, keepdims=True)
    first = N_GROUPS + g_sel * EXPERTS_PER_GROUP
    in_group = (lane >= first) & (lane < first + EXPERTS_PER_GROUP)
    e_max, _ = first_lane_of_max(logits, in_group)
    e_exp = jnp.where(in_group, jnp.exp(logits - e_max), 0.0)
    prob = e_exp / jnp.sum(e_exp, axis=-1, keepdims=True)
    p1, i1 = first_lane_of_max(prob, in_group)
    p2, i2 = first_lane_of_max(prob, in_group & (lane != i1))
    denom = p1 + p2
    record = jnp.where(lane == 0, (i1 - N_GROUPS).astype(F32),
                       jnp.where(lane == 1, (i2 - N_GROUPS).astype(F32),
                                 jnp.where(lane == 2, g_w * p1 / denom,
                                           jnp.where(lane == 3, g_w * p2 / denom, 0.0))))
    route_ref[...] = record
    route_t_ref[...] = record.T[:ROUTE_ROWS, :]


def _mix_out(x2, oa, ob, sga, sgb, wa, wb, wo, g_ffn, wr_hi, wr_lo, br):
    t = x2.shape[0]
    row = lambda i: (i, 0)
    const = lambda i: (0, 0)
    once = dict(pipeline_mode=pl.Buffered(1))
    return pl.pallas_call(
        _mix_out_kernel,
        grid=(t // MIX_TM,),
        in_specs=[pl.BlockSpec((MIX_TM, D_MODEL), row),
                  pl.BlockSpec((MIX_TM, WIDTH), row), pl.BlockSpec((MIX_TM, WIDTH), row),
                  pl.BlockSpec((MIX_TM, D_MODEL), row), pl.BlockSpec((MIX_TM, D_MODEL), row),
                  pl.BlockSpec((WIDTH, D_MODEL), const, **once), pl.BlockSpec((WIDTH, D_MODEL), const, **once),
                  pl.BlockSpec((D_MODEL, D_MODEL), const, **once), pl.BlockSpec((1, D_MODEL), const),
                  pl.BlockSpec((D_MODEL, LANES), const, **once), pl.BlockSpec((D_MODEL, LANES), const, **once),
                  pl.BlockSpec((1, LANES), const)],
        out_specs=[pl.BlockSpec((MIX_TM, D_MODEL), row), pl.BlockSpec((MIX_TM, D_MODEL), row),
                   pl.BlockSpec((MIX_TM, LANES), row), pl.BlockSpec((ROUTE_ROWS, MIX_TM), lambda i: (0, i))],
        out_shape=[jax.ShapeDtypeStruct((t, D_MODEL), F32), jax.ShapeDtypeStruct((t, D_MODEL), F32),
                   jax.ShapeDtypeStruct((t, LANES), F32), jax.ShapeDtypeStruct((ROUTE_ROWS, t), F32)],
        compiler_params=_cparams("parallel"),
        name="mix_out",
    )(x2, oa, ob, sga, sgb, wa, wb, wo, g_ffn, wr_hi, wr_lo, br)


PLAN_TM = 512


def _moe_plan_kernel(route_t_ref, dest_ref, counts_ref, count_ref, start_ref):
    phase = pl.program_id(0)
    step = pl.program_id(1)
    expert = lax.broadcasted_iota(I32, (LANES, PLAN_TM), 0)
    e1 = route_t_ref[0:1, :].astype(I32)
    e2 = route_t_ref[1:2, :].astype(I32)
    hot1 = expert == e1
    hot2 = expert == e2
    hot = jnp.where(hot1 | hot2, 1.0, 0.0)

    @pl.when((phase == 0) & (step == 0))
    def _():
        count_ref[...] = jnp.zeros_like(count_ref)

    @pl.when(phase == 0)
    def _():
        count_ref[...] += jnp.sum(hot, axis=-1, keepdims=True)
        dest_ref[...] = jnp.zeros_like(dest_ref)

    @pl.when((phase == 1) & (step == 0))
    def _():
        counts = jnp.broadcast_to(count_ref[...], (LANES, LANES))
        counts_ref[...] = counts
        padded = jnp.ceil(counts / DISPATCH_BLOCK) * DISPATCH_BLOCK
        sub = lax.broadcasted_iota(I32, (LANES, LANES), 0)
        ends = padded
        shift = 1
        while shift < LANES:
            ends = ends + jnp.where(sub >= shift, pltpu.roll(ends, shift, 0), 0.0)
            shift *= 2
        start_ref[...] = (ends - padded)[:, 0:1]
        count_ref[...] = jnp.zeros_like(count_ref)

    @pl.when(phase == 1)
    def _():
        tok_r = lax.broadcasted_iota(I32, (PLAN_TM, PLAN_TM), 0)
        tok_c = lax.broadcasted_iota(I32, (PLAN_TM, PLAN_TM), 1)
        earlier = jnp.where(tok_r < tok_c, 1.0, 0.0).astype(BF16)
        slot = start_ref[...] + count_ref[...] + _dot(hot.astype(BF16), earlier)
        d1 = jnp.sum(jnp.where(hot1, slot, 0.0), axis=0, keepdims=True)
        d2 = jnp.sum(jnp.where(hot2, slot, 0.0), axis=0, keepdims=True)
        sub = lax.broadcasted_iota(I32, (ROUTE_ROWS, PLAN_TM), 0)
        dest_ref[...] = jnp.where(sub == 0, d1, jnp.where(sub == 1, d2, 0.0)).astype(I32)
        count_ref[...] += jnp.sum(hot, axis=-1, keepdims=True)


def _moe_plan(route_t):
    t = route_t.shape[1]
    return pl.pallas_call(
        _moe_plan_kernel,
        grid=(2, t // PLAN_TM),
        in_specs=[pl.BlockSpec((ROUTE_ROWS, PLAN_TM), lambda ph, i: (0, i))],
        out_specs=[pl.BlockSpec((ROUTE_ROWS, PLAN_TM), lambda ph, i: (0, i * ph)),
                   pl.BlockSpec((LANES, LANES), lambda ph, i: (0, 0))],
        out_shape=[jax.ShapeDtypeStruct((ROUTE_ROWS, t), I32), jax.ShapeDtypeStruct((LANES, LANES), F32)],
        scratch_shapes=[pltpu.VMEM((LANES, 1), F32), pltpu.VMEM((LANES, 1), F32)],
        compiler_params=_cparams("arbitrary", "arbitrary"),
        name="moe_plan",
    )(route_t)


DISP_TM = 1024


def _moe_dispatch_kernel(d1_ref, d2_ref, xs_ref, zeros_ref, out_ref, sem):
    del zeros_ref

    def row_copy(r, dest):
        return pltpu.make_async_copy(xs_ref.at[pl.ds(r, 1), :], out_ref.at[pl.ds(dest, 1), :], sem)

    def issue(r, carry):
        row_copy(r, d1_ref[r]).start()
        row_copy(r, d2_ref[r]).start()
        return carry
    lax.fori_loop(0, DISP_TM, issue, 0)

    for _ in range(2):
        pltpu.make_async_copy(xs_ref, out_ref.at[pl.ds(0, DISP_TM), :], sem).wait()


def _moe_dispatch(d1, d2, xs, n_rows):
    t = xs.shape[0]
    smem = lambda: pl.BlockSpec((DISP_TM,), lambda i: (i,), memory_space=pltpu.SMEM)
    return pl.pallas_call(
        _moe_dispatch_kernel,
        grid=(t // DISP_TM,),
        in_specs=[smem(), smem(), pl.BlockSpec((DISP_TM, D_MODEL), lambda i: (i, 0)),
                  pl.BlockSpec(memory_space=pl.ANY)],
        out_specs=pl.BlockSpec(memory_space=pl.ANY),
        out_shape=jax.ShapeDtypeStruct((n_rows, D_MODEL), F32),
        scratch_shapes=[pltpu.SemaphoreType.DMA(())],
        input_output_aliases={3: 0},
        compiler_params=_cparams("arbitrary"),
        name="moe_dispatch",
    )(d1, d2, xs, jnp.zeros((n_rows, D_MODEL), F32))


def _moe_expert_kernel(block_expert_ref, n_used_ref, xs_ref, w1_ref, w3_ref, w2_ref, ys_ref):
    del block_expert_ref
    live = pl.program_id(0) < n_used_ref[0]

    @pl.when(live)
    def _():
        x = xs_ref[...].astype(BF16)
        h1 = _dot(x, w1_ref[0])
        h3 = _dot(x, w3_ref[0])
        hid = h1 / (1.0 + jnp.exp(-h1)) * h3
        ys_ref[...] = _dot(hid.astype(BF16), w2_ref[0])

    @pl.when(jnp.logical_not(live))
    def _():
        ys_ref[...] = jnp.zeros_like(ys_ref)


def _moe_experts(block_expert, n_used, xs_sorted, w1, w3, w2):
    n_rows = xs_sorted.shape[0]
    grid_spec = pltpu.PrefetchScalarGridSpec(
        num_scalar_prefetch=2,
        grid=(n_rows // DISPATCH_BLOCK,),
        in_specs=[pl.BlockSpec((DISPATCH_BLOCK, D_MODEL), lambda j, be, nu: (j, 0)),
                  pl.BlockSpec((1, D_MODEL, EXPERT_FF), lambda j, be, nu: (be[j], 0, 0)),
                  pl.BlockSpec((1, D_MODEL, EXPERT_FF), lambda j, be, nu: (be[j], 0, 0)),
                  pl.BlockSpec((1, EXPERT_FF, D_MODEL), lambda j, be, nu: (be[j], 0, 0))],
        out_specs=pl.BlockSpec((DISPATCH_BLOCK, D_MODEL), lambda j, be, nu: (j, 0)))
    return pl.pallas_call(
        _moe_expert_kernel,
        grid_spec=grid_spec,
        out_shape=jax.ShapeDtypeStruct((n_rows, D_MODEL), F32),
        compiler_params=_cparams("arbitrary"),
        name="moe_experts",
    )(block_expert, n_used, xs_sorted, w1, w3, w2)


COMB_TM = 512


def _moe_combine_kernel(d1_ref, d2_ref, x1_ref, route_ref, g_ref, ys_ref, out_ref, y1_ref, y2_ref, sem):
    def row_copy(src, r, buf):
        return pltpu.make_async_copy(ys_ref.at[pl.ds(src, 1), :], buf.at[pl.ds(r, 1), :], sem)

    def issue(r, carry):
        row_copy(d1_ref[r], r, y1_ref).start()
        row_copy(d2_ref[r], r, y2_ref).start()
        return carry
    lax.fori_loop(0, COMB_TM, issue, 0)

    for buf in (y1_ref, y2_ref):
        pltpu.make_async_copy(ys_ref.at[pl.ds(0, COMB_TM), :], buf, sem).wait()

    route = route_ref[...]
    x2 = x1_ref[...] + (route[:, 2:3] * y1_ref[...] + route[:, 3:4] * y2_ref[...])
    out_ref[...] = x2 * lax.rsqrt(jnp.mean(x2 * x2, axis=-1, keepdims=True) + RMS_EPS) * g_ref[...]


def _moe_combine(d1, d2, x1, route, g_final, ys):
    t = x1.shape[0]
    smem = lambda: pl.BlockSpec((COMB_TM,), lambda i: (i,), memory_space=pltpu.SMEM)
    row = lambda i: (i, 0)
    return pl.pallas_call(
        _moe_combine_kernel,
        grid=(t // COMB_TM,),
        in_specs=[smem(), smem(), pl.BlockSpec((COMB_TM, D_MODEL), row), pl.BlockSpec((COMB_TM, LANES), row),
                  pl.BlockSpec((1, D_MODEL), lambda i: (0, 0)), pl.BlockSpec(memory_space=pl.ANY)],
        out_specs=pl.BlockSpec((COMB_TM, D_MODEL), row),
        out_shape=jax.ShapeDtypeStruct((t, D_MODEL), F32),
        scratch_shapes=[pltpu.VMEM((COMB_TM, D_MODEL), F32), pltpu.VMEM((COMB_TM, D_MODEL), F32),
                        pltpu.SemaphoreType.DMA(())],
        compiler_params=_cparams("arbitrary"),
        name="moe_combine",
    )(d1, d2, x1, route, g_final, ys)


def _hier_moe_tail(x1, xs, route, route_t, w1, w3, w2, g_final):
    t = x1.shape[0]
    dest, counts = _moe_plan(route_t)
    counts = counts[:N_EXPERTS, 0].astype(I32)
    padded = (counts + DISPATCH_BLOCK - 1) // DISPATCH_BLOCK * DISPATCH_BLOCK
    ends = jnp.cumsum(padded)
    n_blocks = (t * 2) // DISPATCH_BLOCK + N_EXPERTS
    block_start = jnp.arange(n_blocks, dtype=I32) * DISPATCH_BLOCK
    block_expert = jnp.minimum(jnp.sum((ends[None, :] <= block_start[:, None]).astype(I32), axis=1), N_EXPERTS - 1)
    n_used = (ends[-1:] // DISPATCH_BLOCK).astype(I32)
    d1, d2 = dest[0], dest[1]
    xs_sorted = _moe_dispatch(d1, d2, xs, n_blocks * DISPATCH_BLOCK)
    ys = _moe_experts(block_expert, n_used, xs_sorted, w1, w3, w2)
    return _moe_combine(d1, d2, x1, route, g_final, ys)


def kernel(x, g_mix, w_in, w_proj_a, w_proj_b, w_out, g_ffn, w_group, b_group, w_expert, b_expert, w1, w3, w2, g_final):
    batch, seq, d = x.shape
    assert d == D_MODEL and g_mix.shape[0] == 1, "one layer of width D_MODEL"
    x2 = x.reshape(batch * seq, d)
    cos_t, sin_t = _rope_tables(seq)
    qa, ka, va, qb, kb, vb, qi, ki2, wi, sga, sgb, kmean = _in_proj(
        x2, g_mix[0][None, :], _arrange_w_in(w_in[0]), cos_t, sin_t, seq)
    selb = _moba_select(qa, kmean, batch, seq)
    oa = _moba_attn(qa, ka, va, selb, batch, seq)
    ob = _dsa_attend(qb, kb, vb, _dsa_select(qi, ki2, wi, batch, seq), batch, seq)
    wr_hi, wr_lo, br = _router_params(w_group[0], b_group[0], w_expert[0], b_expert[0])
    x1, xs, route, route_t = _mix_out(
        x2, oa, ob, sga, sgb, w_proj_a[0].astype(BF16), w_proj_b[0].astype(BF16), w_out[0].astype(BF16),
        g_ffn[0][None, :], wr_hi, wr_lo, br)
    out = _hier_moe_tail(x1, xs, route, route_t, w1[0].astype(BF16), w3[0].astype(BF16), w2[0].astype(BF16),
                         g_final[None, :])
    return out.reshape(batch, seq, d)


def _router_params(w_group, b_group, w_expert, b_expert):
    w = jnp.pad(jnp.concatenate([w_group, w_expert], axis=1), ((0, 0), (0, LANES - N_GROUPS - N_EXPERTS)))
    b = jnp.pad(jnp.concatenate([b_group, b_expert]), (0, LANES - N_GROUPS - N_EXPERTS))[None, :]
    w_hi = w.astype(BF16)
    w_lo = (w - w_hi.astype(F32)).astype(BF16)
    return w_hi, w_lo, b
```

```python
import functools

import jax
import jax.numpy as jnp
from jax import lax
from jax.experimental import pallas as pl
from jax.experimental.pallas import tpu as pltpu

F32 = jnp.float32
BF16 = jnp.bfloat16
I32 = jnp.int32

D_MODEL = 1024
HEAD_DIM = 64
N_HEADS = 8
WIDTH = N_HEADS * HEAD_DIM
N_PAIRS = N_HEADS // 2
MOBA_BLOCK = 256
MOBA_TOPK = 3
DSA_TOPK = 256
IDX_SCALE = float(WIDTH) ** -0.5
ATTN_SCALE = float(HEAD_DIM) ** -0.5
N_GROUPS = 4
EXPERTS_PER_GROUP = 8
N_EXPERTS = N_GROUPS * EXPERTS_PER_GROUP
EXPERT_FF = 512
DISPATCH_BLOCK = 256
ROPE_THETA = 10000.0
RMS_EPS = 1e-6
NEG = -1e30

LANES = 128
VMEM_LIMIT = 56 * 1024 * 1024

C_QA, C_KA, C_VA, C_QB, C_KB, C_VB, C_QI = (i * WIDTH for i in range(7))
C_KI = 7 * WIDTH
C_WI = C_KI + LANES
C_GA = C_WI + WIDTH
C_GB = C_GA + D_MODEL
IN_COLS_PADDED = C_GB + D_MODEL


def _cparams(*semantics):
    return pltpu.CompilerParams(dimension_semantics=semantics, vmem_limit_bytes=VMEM_LIMIT)


def _dot_nt(a, b):
    return lax.dot_general(a, b, (((1,), (1,)), ((), ())), preferred_element_type=F32)


def _dot(a, b):
    return jnp.dot(a, b, preferred_element_type=F32)


IN_TM = 512


def _in_proj_kernel(x_ref, g_ref, w_ref, cos_ref, sin_ref,
                    qa_ref, ka_ref, va_ref, qb_ref, kb_ref, vb_ref, qi_ref, ki_ref, wi_ref,
                    sga_ref, sgb_ref, kmean_ref):
    x = x_ref[...]
    h = x * lax.rsqrt(jnp.mean(x * x, axis=-1, keepdims=True) + RMS_EPS) * g_ref[...]
    hb = h.astype(BF16)
    cos = cos_ref[...]
    sin = sin_ref[...]
    upper_half = (lax.broadcasted_iota(I32, (IN_TM, LANES), 1) & (HEAD_DIM // 2)) != 0

    def rope(v):
        partner = jnp.where(upper_half, pltpu.roll(v, HEAD_DIM // 2, 1), pltpu.roll(v, LANES - HEAD_DIM // 2, 1))
        return v * cos + partner * sin

    def proj(c0, width):
        return _dot(hb, w_ref[:, c0:c0 + width])

    def store_heads(ref, c0, rotary):
        r = proj(c0, WIDTH)
        for j in range(WIDTH // LANES):
            v = r[:, j * LANES:(j + 1) * LANES]
            ref[:, j * LANES:(j + 1) * LANES] = (rope(v) if rotary else v).astype(ref.dtype)

    store_heads(qa_ref, C_QA, True)
    store_heads(va_ref, C_VA, False)
    store_heads(qb_ref, C_QB, True)
    store_heads(kb_ref, C_KB, True)
    store_heads(vb_ref, C_VB, False)
    store_heads(qi_ref, C_QI, True)

    r = proj(C_KA, WIDTH)
    for j in range(WIDTH // LANES):
        v = rope(r[:, j * LANES:(j + 1) * LANES])
        ka_ref[:, j * LANES:(j + 1) * LANES] = v.astype(BF16)
        for blk in range(IN_TM // MOBA_BLOCK):
            kmean_ref[blk, :, j * LANES:(j + 1) * LANES] = jnp.mean(
                v[blk * MOBA_BLOCK:(blk + 1) * MOBA_BLOCK], axis=0, keepdims=True)

    ki_ref[...] = rope(proj(C_KI, LANES)).astype(BF16)
    wi_ref[...] = proj(C_WI, WIDTH) * IDX_SCALE
    sga_ref[...] = (1.0 / (1.0 + jnp.exp(-proj(C_GA, D_MODEL)))).astype(BF16)
    sgb_ref[...] = (1.0 / (1.0 + jnp.exp(-proj(C_GB, D_MODEL)))).astype(BF16)


def _in_proj(x2, g_mix, w_r, cos_t, sin_t, seq):
    t = x2.shape[0]
    n_tiles = t // IN_TM
    tiles_per_seq = seq // IN_TM
    row = lambda i: (i, 0)
    const = lambda i: (0, 0)
    act = lambda w, dt: jax.ShapeDtypeStruct((t, w), dt)
    out_shape = ([act(WIDTH, BF16)] * 7 + [act(LANES, BF16), act(WIDTH, F32), act(D_MODEL, BF16), act(D_MODEL, BF16),
                                           jax.ShapeDtypeStruct((t // MOBA_BLOCK, 1, WIDTH), F32)])
    out_specs = ([pl.BlockSpec((IN_TM, WIDTH), row)] * 7
                 + [pl.BlockSpec((IN_TM, LANES), row), pl.BlockSpec((IN_TM, WIDTH), row),
                    pl.BlockSpec((IN_TM, D_MODEL), row), pl.BlockSpec((IN_TM, D_MODEL), row),
                    pl.BlockSpec((IN_TM // MOBA_BLOCK, 1, WIDTH), lambda i: (i, 0, 0))])
    return pl.pallas_call(
        _in_proj_kernel,
        grid=(n_tiles,),
        in_specs=[pl.BlockSpec((IN_TM, D_MODEL), row),
                  pl.BlockSpec((1, D_MODEL), const),
                  pl.BlockSpec((D_MODEL, IN_COLS_PADDED), const, pipeline_mode=pl.Buffered(1)),
                  pl.BlockSpec((IN_TM, LANES), lambda i: (i % tiles_per_seq, 0)),
                  pl.BlockSpec((IN_TM, LANES), lambda i: (i % tiles_per_seq, 0))],
        out_specs=out_specs,
        out_shape=out_shape,
        compiler_params=_cparams("parallel"),
        name="in_proj",
    )(x2, g_mix, w_r, cos_t, sin_t)


def _moba_select_kernel(qa_ref, km_ref, selb_ref):
    own = pl.program_id(1)
    n_blocks = km_ref.shape[1]
    km = km_ref[0]
    col_head = lax.broadcasted_iota(I32, (N_HEADS, WIDTH), 1) // HEAD_DIM
    head_mask = col_head == lax.broadcasted_iota(I32, (N_HEADS, WIDTH), 0)
    rows = [jnp.where(head_mask, km[n:n + 1, :], 0.0) for n in range(n_blocks)]
    rows.append(jnp.zeros((LANES - n_blocks * N_HEADS, WIDTH), F32))
    km_t = jnp.concatenate(rows, axis=0).astype(BF16)
    gate = _dot_nt(qa_ref[...], km_t)
    lane = lax.broadcasted_iota(I32, gate.shape, 1)
    past = (lane // N_HEADS) < own
    g = jnp.where(past, gate, NEG)
    rank = jnp.zeros(gate.shape, I32)
    for r in range(1, n_blocks):
        later = pltpu.roll(g, LANES - N_HEADS * r, 1)
        earlier = pltpu.roll(g, N_HEADS * r, 1)
        rank = rank + (later > g).astype(I32) + (earlier >= g).astype(I32)
    chosen = jnp.where(past & (rank < MOBA_TOPK), 1.0, 0.0).astype(BF16)
    src = lax.broadcasted_iota(I32, (LANES, N_PAIRS * LANES), 0)
    dst = lax.broadcasted_iota(I32, (LANES, N_PAIRS * LANES), 1)
    src_n, src_h = src // N_HEADS, src % N_HEADS
    dst_pair, dst_w = dst // LANES, dst % LANES
    expand = ((src_n < n_blocks) & (dst_w < 2 * N_HEADS) & (dst_w % N_HEADS == src_n)
              & (dst_pair * 2 + dst_w // N_HEADS == src_h))
    hit = _dot(chosen, jnp.where(expand, 1.0, 0.0).astype(BF16))
    selb_ref[...] = jnp.where(hit > 0.5, 0.0, NEG)


def _moba_select(qa, kmean, batch, seq):
    n_blocks = seq // MOBA_BLOCK
    assert n_blocks * N_HEADS <= LANES and n_blocks <= N_HEADS
    t = batch * seq
    return pl.pallas_call(
        _moba_select_kernel,
        grid=(batch, n_blocks),
        in_specs=[pl.BlockSpec((MOBA_BLOCK, WIDTH), lambda b, i: (b * n_blocks + i, 0)),
                  pl.BlockSpec((1, n_blocks, WIDTH), lambda b, i: (b, 0, 0))],
        out_specs=pl.BlockSpec((MOBA_BLOCK, N_PAIRS * LANES), lambda b, i: (b * n_blocks + i, 0)),
        out_shape=jax.ShapeDtypeStruct((t, N_PAIRS * LANES), F32),
        compiler_params=_cparams("parallel", "parallel"),
        name="moba_select",
    )(qa, kmean.reshape(batch, n_blocks, WIDTH))


def _stack_heads(q2):
    lane = lax.broadcasted_iota(I32, q2.shape, 1)
    zero = jnp.zeros_like(q2)
    return jnp.concatenate([jnp.where(lane < HEAD_DIM, q2, zero), jnp.where(lane >= HEAD_DIM, q2, zero)], axis=0)


def _unstack_heads(o):
    rows = o.shape[0] // 2
    lane = lax.broadcasted_iota(I32, (rows, LANES), 1)
    return jnp.where(lane < HEAD_DIM, o[:rows], o[rows:])


def _masked_attention(s, v):
    p = jnp.exp(s - jnp.max(s, axis=-1, keepdims=True))
    return _dot(p.astype(BF16), v) / jnp.sum(p, axis=-1, keepdims=True)


PAIRS_PER_STEP = 2
STEP_LANES = PAIRS_PER_STEP * LANES


def _moba_attn_kernel(q_ref, k_ref, v_ref, selb_ref, o_ref):
    own = pl.program_id(2)
    n_blocks = k_ref.shape[0] // MOBA_BLOCK
    q_pos = lax.broadcasted_iota(I32, (2 * MOBA_BLOCK, MOBA_BLOCK), 0) % MOBA_BLOCK
    causal = lax.broadcasted_iota(I32, (2 * MOBA_BLOCK, MOBA_BLOCK), 1) <= q_pos
    for i in range(n_blocks):
        @pl.when(own == i)
        def _(i=i):
            n_keys = (i + 1) * MOBA_BLOCK
            for u in range(PAIRS_PER_STEP):
                pair = slice(u * LANES, (u + 1) * LANES)
                selb = selb_ref[:, pair]
                s = _dot_nt(_stack_heads(q_ref[:, pair]), k_ref[0:n_keys, pair])
                parts = []
                for j in range(i):
                    bias = jnp.concatenate([selb[:, j:j + 1], selb[:, N_HEADS + j:N_HEADS + j + 1]], axis=0)
                    parts.append(s[:, j * MOBA_BLOCK:(j + 1) * MOBA_BLOCK] + bias)
                parts.append(jnp.where(causal, s[:, i * MOBA_BLOCK:], NEG))
                o = _masked_attention(jnp.concatenate(parts, axis=1), v_ref[0:n_keys, pair])
                o_ref[:, pair] = _unstack_heads(o).astype(o_ref.dtype)


def _moba_attn(qa, ka, va, selb, batch, seq):
    n_blocks = seq // MOBA_BLOCK
    t = batch * seq
    q_map = lambda b, p, i: (b * n_blocks + i, p)
    kv_map = lambda b, p, i: (b, p)
    return pl.pallas_call(
        _moba_attn_kernel,
        grid=(batch, N_PAIRS // PAIRS_PER_STEP, n_blocks),
        in_specs=[pl.BlockSpec((MOBA_BLOCK, STEP_LANES), q_map),
                  pl.BlockSpec((seq, STEP_LANES), kv_map),
                  pl.BlockSpec((seq, STEP_LANES), kv_map),
                  pl.BlockSpec((MOBA_BLOCK, STEP_LANES), q_map)],
        out_specs=pl.BlockSpec((MOBA_BLOCK, STEP_LANES), q_map),
        out_shape=jax.ShapeDtypeStruct((t, WIDTH), BF16),
        compiler_params=_cparams("parallel", "parallel", "arbitrary"),
        name="moba_attn",
    )(qa, ka, va, selb)


def _rope_tables(seq):
    inv = jnp.power(ROPE_THETA, -jnp.arange(0, HEAD_DIM, 2, dtype=F32) / HEAD_DIM)
    ang = jnp.arange(seq, dtype=F32)[:, None] * inv[None, :]
    cos, sin = jnp.cos(ang), jnp.sin(ang)
    reps = LANES // HEAD_DIM
    return jnp.tile(jnp.concatenate([cos, cos], axis=-1), (1, reps)), jnp.tile(jnp.concatenate([-sin, sin], axis=-1), (1, reps))


def _arrange_w_in(w_in):
    sizes = (WIDTH,) * 7 + (HEAD_DIM, N_HEADS, D_MODEL, D_MODEL)
    offs = [0]
    for s in sizes:
        offs.append(offs[-1] + s)
    seg = [w_in[:, offs[i]:offs[i + 1]] for i in range(len(sizes))]
    qa, ka, va, qb, kb, vb, qi, ki, wi, ga, gb = seg
    wi_pairs = jnp.pad(wi.reshape(-1, N_PAIRS, 2), ((0, 0), (0, 0), (0, LANES - 2))).reshape(-1, WIDTH)
    w = jnp.concatenate([qa * ATTN_SCALE, ka, va, qb * ATTN_SCALE, kb, vb, qi, ki, ki, wi_pairs, ga, gb], axis=1)
    return w.astype(BF16)


DSA_TQ = 256
DSA_ROWS = 64
INT_MIN = -(2 ** 31)


def _count_lanes(hit):
    acc = hit[:, :LANES]
    for j in range(1, hit.shape[1] // LANES):
        acc = acc + hit[:, j * LANES:(j + 1) * LANES]
    return jnp.sum(acc, axis=-1, keepdims=True)


def _dsa_score_pair(i, qi_ref, ki_ref, wi_ref, score_ref):
    n_keys = (i + 1) * DSA_TQ
    logit = _dot_nt(_stack_heads(qi_ref[...]), ki_ref[0:n_keys, :])
    wi = wi_ref[...]
    part = wi[:, 0:1] * jnp.maximum(logit[:DSA_TQ], 0.0) + wi[:, 1:2] * jnp.maximum(logit[DSA_TQ:], 0.0)
    pair = pl.program_id(2)

    @pl.when(pair == 0)
    def _():
        score_ref[:, 0:n_keys] = part

    @pl.when(pair > 0)
    def _():
        score_ref[:, 0:n_keys] += part


def _dsa_select_chunk(i, score_ref, bias_ref):
    n_keys = (i + 1) * DSA_TQ
    row = lax.broadcasted_iota(I32, (DSA_TQ, DSA_TQ), 0)
    col = lax.broadcasted_iota(I32, (DSA_TQ, DSA_TQ), 1)
    score_ref[:, i * DSA_TQ:n_keys] = jnp.where(col <= row, score_ref[:, i * DSA_TQ:n_keys], NEG)

    def as_float(code):
        return pltpu.bitcast(code ^ ((code >> 31) & 0x7FFFFFFF), F32)

    def count_ge(rg, cand):
        k = score_ref[rg * DSA_ROWS:(rg + 1) * DSA_ROWS, 0:n_keys]
        return _count_lanes(jnp.where(k >= as_float(cand), 1.0, 0.0))

    n_rg = DSA_TQ // DSA_ROWS

    def try_bit(rg, t_old, n_old, cand):
        n = count_ge(rg, cand)
        keep = n >= DSA_TOPK
        return jnp.where(keep, cand, t_old), jnp.where(keep, n, n_old)

    def unsettled(ns):
        return jnp.max(jnp.abs(jnp.concatenate(ns, axis=0) - DSA_TOPK)) > 0.0

    def bit_cond(state):
        ib, _, _, more = state
        return (ib < 31) & more

    def bit_body(state):
        ib, ts, ns, _ = state
        bit = jnp.left_shift(jnp.int32(1), 30 - ib)
        pairs = [try_bit(rg, ts[rg], ns[rg], ts[rg] | bit) for rg in range(n_rg)]
        ts, ns = tuple(p[0] for p in pairs), tuple(p[1] for p in pairs)
        return ib + 1, ts, ns, unsettled(ns)

    pairs = [try_bit(rg, jnp.full((DSA_ROWS, 1), INT_MIN, I32), jnp.full((DSA_ROWS, 1), float(n_keys), F32),
                     jnp.zeros((DSA_ROWS, 1), I32)) for rg in range(n_rg)]
    ts, ns = tuple(p[0] for p in pairs), tuple(p[1] for p in pairs)
    _, ts, _, _ = lax.while_loop(bit_cond, bit_body, (jnp.int32(0), ts, ns, unsettled(ns)))
    code = jnp.concatenate(ts, axis=0)
    thr = jnp.where(code == INT_MIN, -jnp.inf, as_float(code))

    keys = score_ref[:, 0:n_keys]
    need = float(DSA_TOPK) - _count_lanes(jnp.where(keys > thr, 1.0, 0.0))
    strictly_before = jnp.where(row < col, 1.0, 0.0).astype(BF16)
    ties_seen = jnp.zeros((DSA_TQ, 1), F32)
    for t in range(i + 1):
        k = keys[:, t * DSA_TQ:(t + 1) * DSA_TQ]
        tie = jnp.where(k == thr, 1.0, 0.0)
        ties_before = ties_seen + _dot(tie.astype(BF16), strictly_before)
        chosen = (k > thr) | ((k == thr) & (ties_before < need))
        if t == i:
            chosen = chosen & (col <= row)
        bias_ref[:, t * DSA_TQ:(t + 1) * DSA_TQ] = jnp.where(chosen, 0.0, NEG).astype(bias_ref.dtype)
        ties_seen = ties_seen + jnp.sum(tie, axis=-1, keepdims=True)
    if n_keys < bias_ref.shape[1]:
        bias_ref[:, n_keys:] = jnp.full((DSA_TQ, bias_ref.shape[1] - n_keys), NEG, bias_ref.dtype)


def _dsa_select_kernel(qi_ref, ki_ref, wi_ref, bias_ref, score_ref):
    for i in range(ki_ref.shape[0] // DSA_TQ):
        @pl.when(pl.program_id(1) == i)
        def _(i=i):
            _dsa_score_pair(i, qi_ref, ki_ref, wi_ref, score_ref)
            pl.when(pl.program_id(2) == N_PAIRS - 1)(functools.partial(_dsa_select_chunk, i, score_ref, bias_ref))


def _dsa_select(qi, ki2, wi, batch, seq):
    assert seq % DSA_TQ == 0 and min(DSA_TOPK, seq // 4) == DSA_TOPK
    n_chunks = seq // DSA_TQ
    q_map = lambda b, c, p: (b * n_chunks + c, p)
    return pl.pallas_call(
        _dsa_select_kernel,
        grid=(batch, n_chunks, N_PAIRS),
        in_specs=[pl.BlockSpec((DSA_TQ, LANES), q_map),
                  pl.BlockSpec((seq, LANES), lambda b, c, p: (b, 0)),
                  pl.BlockSpec((DSA_TQ, LANES), q_map)],
        out_specs=pl.BlockSpec((DSA_TQ, seq), lambda b, c, p: (b * n_chunks + c, 0)),
        out_shape=jax.ShapeDtypeStruct((batch * seq, seq), BF16),
        scratch_shapes=[pltpu.VMEM((DSA_TQ, seq), F32)],
        compiler_params=_cparams("parallel", "arbitrary", "arbitrary"),
        name="dsa_select",
    )(qi, ki2, wi)


def _dsa_attend_kernel(q_ref, k_ref, v_ref, bias_ref, o_ref):
    for i in range(k_ref.shape[0] // DSA_TQ):
        @pl.when(pl.program_id(1) == i)
        def _(i=i):
            n_keys = (i + 1) * DSA_TQ
            bias = bias_ref[:, 0:n_keys].astype(F32)
            bias = jnp.concatenate([bias, bias], axis=0)
            for u in range(PAIRS_PER_STEP):
                pair = slice(u * LANES, (u + 1) * LANES)
                s = _dot_nt(_stack_heads(q_ref[:, pair]), k_ref[0:n_keys, pair]) + bias
                o_ref[:, pair] = _unstack_heads(_masked_attention(s, v_ref[0:n_keys, pair])).astype(o_ref.dtype)


def _dsa_attend(qb, kb, vb, bias, batch, seq):
    n_chunks = seq // DSA_TQ
    q_map = lambda b, c, p: (b * n_chunks + c, p)
    kv_map = lambda b, c, p: (b, p)
    return pl.pallas_call(
        _dsa_attend_kernel,
        grid=(batch, n_chunks, N_PAIRS // PAIRS_PER_STEP),
        in_specs=[pl.BlockSpec((DSA_TQ, STEP_LANES), q_map),
                  pl.BlockSpec((seq, STEP_LANES), kv_map),
                  pl.BlockSpec((seq, STEP_LANES), kv_map),
                  pl.BlockSpec((DSA_TQ, seq), lambda b, c, p: (b * n_chunks + c, 0))],
        out_specs=pl.BlockSpec((DSA_TQ, STEP_LANES), q_map),
        out_shape=jax.ShapeDtypeStruct((batch * seq, WIDTH), BF16),
        compiler_params=_cparams("parallel", "parallel", "arbitrary"),
        name="dsa_attend",
    )(qb, kb, vb, bias)


MIX_TM = 512
ROUTE_ROWS = 8


def _mix_out_kernel(x_ref, oa_ref, ob_ref, sga_ref, sgb_ref, wa_ref, wb_ref, wo_ref, g_ref, wr_hi_ref, wr_lo_ref, br_ref,
                    x1_ref, xs_ref, route_ref, route_t_ref):
    mixed = (sga_ref[...].astype(F32) * _dot(oa_ref[...], wa_ref[...])
             + sgb_ref[...].astype(F32) * _dot(ob_ref[...], wb_ref[...]))
    x1 = x_ref[...] + _dot(mixed.astype(BF16), wo_ref[...])
    x1_ref[...] = x1
    hn = x1 * lax.rsqrt(jnp.mean(x1 * x1, axis=-1, keepdims=True) + RMS_EPS) * g_ref[...]
    xs_ref[...] = hn

    hi = hn.astype(BF16)
    lo = (hn - hi.astype(F32)).astype(BF16)
    logits = _dot(hi, wr_hi_ref[...]) + _dot(lo, wr_hi_ref[...]) + _dot(hi, wr_lo_ref[...]) + br_ref[...]
    lane = lax.broadcasted_iota(I32, logits.shape, 1)
    far = jnp.int32(LANES)

    def first_lane_of_max(v, valid):
        top = jnp.max(jnp.where(valid, v, NEG), axis=-1, keepdims=True)
        return top, jnp.min(jnp.where(valid & (v == top), lane, far), axis=-1, keepdims=True)

    is_group = lane < N_GROUPS
    g_max, g_sel = first_lane_of_max(logits, is_group)
    g_w = 1.0 / jnp.sum(jnp.where(is_group, jnp.exp(logits - g_max), 0.0), axis=-1, keepdims=True)
    first = N_GROUPS + g_sel * EXPERTS_PER_GROUP
    in_group = (lane >= first) & (lane < first + EXPERTS_PER_GROUP)
    e_max, _ = first_lane_of_max(logits, in_group)
    e_exp = jnp.where(in_group, jnp.exp(logits - e_max), 0.0)
    prob = e_exp / jnp.sum(e_exp, axis=-1, keepdims=True)
    p1, i1 = first_lane_of_max(prob, in_group)
    p2, i2 = first_lane_of_max(prob, in_group & (lane != i1))
    denom = p1 + p2
    record = jnp.where(lane == 0, (i1 - N_GROUPS).astype(F32),
                       jnp.where(lane == 1, (i2 - N_GROUPS).astype(F32),
                                 jnp.where(lane == 2, g_w * p1 / denom,
                                           jnp.where(lane == 3, g_w * p2 / denom, 0.0))))
    route_ref[...] = record
    route_t_ref[...] = record.T[:ROUTE_ROWS, :]


def _mix_out(x2, oa, ob, sga, sgb, wa, wb, wo, g_ffn, wr_hi, wr_lo, br):
    t = x2.shape[0]
    row = lambda i: (i, 0)
    const = lambda i: (0, 0)
    once = dict(pipeline_mode=pl.Buffered(1))
    return pl.pallas_call(
        _mix_out_kernel,
        grid=(t // MIX_TM,),
        in_specs=[pl.BlockSpec((MIX_TM, D_MODEL), row),
                  pl.BlockSpec((MIX_TM, WIDTH), row), pl.BlockSpec((MIX_TM, WIDTH), row),
                  pl.BlockSpec((MIX_TM, D_MODEL), row), pl.BlockSpec((MIX_TM, D_MODEL), row),
                  pl.BlockSpec((WIDTH, D_MODEL), const, **once), pl.BlockSpec((WIDTH, D_MODEL), const, **once),
                  pl.BlockSpec((D_MODEL, D_MODEL), const, **once), pl.BlockSpec((1, D_MODEL), const),
                  pl.BlockSpec((D_MODEL, LANES), const, **once), pl.BlockSpec((D_MODEL, LANES), const, **once),
                  pl.BlockSpec((1, LANES), const)],
        out_specs=[pl.BlockSpec((MIX_TM, D_MODEL), row), pl.BlockSpec((MIX_TM, D_MODEL), row),
                   pl.BlockSpec((MIX_TM, LANES), row), pl.BlockSpec((ROUTE_ROWS, MIX_TM), lambda i: (0, i))],
        out_shape=[jax.ShapeDtypeStruct((t, D_MODEL), F32), jax.ShapeDtypeStruct((t, D_MODEL), F32),
                   jax.ShapeDtypeStruct((t, LANES), F32), jax.ShapeDtypeStruct((ROUTE_ROWS, t), F32)],
        compiler_params=_cparams("parallel"),
        name="mix_out",
    )(x2, oa, ob, sga, sgb, wa, wb, wo, g_ffn, wr_hi, wr_lo, br)


PLAN_TM = 512


def _moe_plan_kernel(route_t_ref, dest_ref, counts_ref, count_ref, start_ref):
    phase = pl.program_id(0)
    step = pl.program_id(1)
    expert = lax.broadcasted_iota(I32, (LANES, PLAN_TM), 0)
    e1 = route_t_ref[0:1, :].astype(I32)
    e2 = route_t_ref[1:2, :].astype(I32)
    hot1 = expert == e1
    hot2 = expert == e2
    hot = jnp.where(hot1 | hot2, 1.0, 0.0)

    @pl.when((phase == 0) & (step == 0))
    def _():
        count_ref[...] = jnp.zeros_like(count_ref)

    @pl.when(phase == 0)
    def _():
        count_ref[...] += jnp.sum(hot, axis=-1, keepdims=True)
        dest_ref[...] = jnp.zeros_like(dest_ref)

    @pl.when((phase == 1) & (step == 0))
    def _():
        counts = jnp.broadcast_to(count_ref[...], (LANES, LANES))
        counts_ref[...] = counts
        padded = jnp.ceil(counts / DISPATCH_BLOCK) * DISPATCH_BLOCK
        sub = lax.broadcasted_iota(I32, (LANES, LANES), 0)
        ends = padded
        shift = 1
        while shift < LANES:
            ends = ends + jnp.where(sub >= shift, pltpu.roll(ends, shift, 0), 0.0)
            shift *= 2
        start_ref[...] = (ends - padded)[:, 0:1]
        count_ref[...] = jnp.zeros_like(count_ref)

    @pl.when(phase == 1)
    def _():
        tok_r = lax.broadcasted_iota(I32, (PLAN_TM, PLAN_TM), 0)
        tok_c = lax.broadcasted_iota(I32, (PLAN_TM, PLAN_TM), 1)
        earlier = jnp.where(tok_r < tok_c, 1.0, 0.0).astype(BF16)
        slot = start_ref[...] + count_ref[...] + _dot(hot.astype(BF16), earlier)
        d1 = jnp.sum(jnp.where(hot1, slot, 0.0), axis=0, keepdims=True)
        d2 = jnp.sum(jnp.where(hot2, slot, 0.0), axis=0, keepdims=True)
        sub = lax.broadcasted_iota(I32, (ROUTE_ROWS, PLAN_TM), 0)
        dest_ref[...] = jnp.where(sub == 0, d1, jnp.where(sub == 1, d2, 0.0)).astype(I32)
        count_ref[...] += jnp.sum(hot, axis=-1, keepdims=True)


def _moe_plan(route_t):
    t = route_t.shape[1]
    return pl.pallas_call(
        _moe_plan_kernel,
        grid=(2, t // PLAN_TM),
        in_specs=[pl.BlockSpec((ROUTE_ROWS, PLAN_TM), lambda ph, i: (0, i))],
        out_specs=[pl.BlockSpec((ROUTE_ROWS, PLAN_TM), lambda ph, i: (0, i * ph)),
                   pl.BlockSpec((LANES, LANES), lambda ph, i: (0, 0))],
        out_shape=[jax.ShapeDtypeStruct((ROUTE_ROWS, t), I32), jax.ShapeDtypeStruct((LANES, LANES), F32)],
        scratch_shapes=[pltpu.VMEM((LANES, 1), F32), pltpu.VMEM((LANES, 1), F32)],
        compiler_params=_cparams("arbitrary", "arbitrary"),
        name="moe_plan",
    )(route_t)


DISP_TM = 1024
DMA_UNROLL = 8


def _moe_dispatch_kernel(d1_ref, d2_ref, xs_ref, zeros_ref, out_ref, sem):
    del zeros_ref

    def row_copy(r, dest):
        return pltpu.make_async_copy(xs_ref.at[pl.ds(r, 1), :], out_ref.at[pl.ds(dest, 1), :], sem)

    def issue(r, carry):
        row_copy(r, d1_ref[r]).start(priority=0)
        row_copy(r, d2_ref[r]).start(priority=1)
        return carry
    lax.fori_loop(0, DISP_TM, issue, 0, unroll=DMA_UNROLL)

    for _ in range(2):
        pltpu.make_async_copy(xs_ref, out_ref.at[pl.ds(0, DISP_TM), :], sem).wait()


def _moe_dispatch(d1, d2, xs, n_rows):
    t = xs.shape[0]
    smem = lambda: pl.BlockSpec((DISP_TM,), lambda i: (i,), memory_space=pltpu.SMEM)
    return pl.pallas_call(
        _moe_dispatch_kernel,
        grid=(t // DISP_TM,),
        in_specs=[smem(), smem(), pl.BlockSpec((DISP_TM, D_MODEL), lambda i: (i, 0)),
                  pl.BlockSpec(memory_space=pl.ANY)],
        out_specs=pl.BlockSpec(memory_space=pl.ANY),
        out_shape=jax.ShapeDtypeStruct((n_rows, D_MODEL), F32),
        scratch_shapes=[pltpu.SemaphoreType.DMA(())],
        input_output_aliases={3: 0},
        compiler_params=_cparams("arbitrary"),
        name="moe_dispatch",
    )(d1, d2, xs, jnp.zeros((n_rows, D_MODEL), F32))


def _moe_expert_kernel(block_expert_ref, n_used_ref, xs_ref, w1_ref, w3_ref, w2_ref, ys_ref):
    del block_expert_ref
    live = pl.program_id(0) < n_used_ref[0]

    @pl.when(live)
    def _():
        x = xs_ref[...].astype(BF16)
        h1 = _dot(x, w1_ref[0])
        h3 = _dot(x, w3_ref[0])
        hid = h1 / (1.0 + jnp.exp(-h1)) * h3
        ys_ref[...] = _dot(hid.astype(BF16), w2_ref[0])

    @pl.when(jnp.logical_not(live))
    def _():
        ys_ref[...] = jnp.zeros_like(ys_ref)


def _moe_experts(block_expert, n_used, xs_sorted, w1, w3, w2):
    n_rows = xs_sorted.shape[0]
    grid_spec = pltpu.PrefetchScalarGridSpec(
        num_scalar_prefetch=2,
        grid=(n_rows // DISPATCH_BLOCK,),
        in_specs=[pl.BlockSpec((DISPATCH_BLOCK, D_MODEL), lambda j, be, nu: (j, 0)),
                  pl.BlockSpec((1, D_MODEL, EXPERT_FF), lambda j, be, nu: (be[j], 0, 0)),
                  pl.BlockSpec((1, D_MODEL, EXPERT_FF), lambda j, be, nu: (be[j], 0, 0)),
                  pl.BlockSpec((1, EXPERT_FF, D_MODEL), lambda j, be, nu: (be[j], 0, 0))],
        out_specs=pl.BlockSpec((DISPATCH_BLOCK, D_MODEL), lambda j, be, nu: (j, 0)))
    return pl.pallas_call(
        _moe_expert_kernel,
        grid_spec=grid_spec,
        out_shape=jax.ShapeDtypeStruct((n_rows, D_MODEL), F32),
        compiler_params=_cparams("arbitrary"),
        name="moe_experts",
    )(block_expert, n_used, xs_sorted, w1, w3, w2)


COMB_TM = 512


def _moe_combine_kernel(d1_ref, d2_ref, x1_ref, route_ref, g_ref, ys_ref, out_ref, y1_ref, y2_ref, sem):
    def row_copy(src, r, buf):
        return pltpu.make_async_copy(ys_ref.at[pl.ds(src, 1), :], buf.at[pl.ds(r, 1), :], sem)

    def issue(r, carry):
        row_copy(d1_ref[r], r, y1_ref).start(priority=0)
        row_copy(d2_ref[r], r, y2_ref).start(priority=1)
        return carry
    lax.fori_loop(0, COMB_TM, issue, 0, unroll=DMA_UNROLL)

    for buf in (y1_ref, y2_ref):
        pltpu.make_async_copy(ys_ref.at[pl.ds(0, COMB_TM), :], buf, sem).wait()

    route = route_ref[...]
    x2 = x1_ref[...] + (route[:, 2:3] * y1_ref[...] + route[:, 3:4] * y2_ref[...])
    out_ref[...] = x2 * lax.rsqrt(jnp.mean(x2 * x2, axis=-1, keepdims=True) + RMS_EPS) * g_ref[...]


def _moe_combine(d1, d2, x1, route, g_final, ys):
    t = x1.shape[0]
    smem = lambda: pl.BlockSpec((COMB_TM,), lambda i: (i,), memory_space=pltpu.SMEM)
    row = lambda i: (i, 0)
    return pl.pallas_call(
        _moe_combine_kernel,
        grid=(t // COMB_TM,),
        in_specs=[smem(), smem(), pl.BlockSpec((COMB_TM, D_MODEL), row), pl.BlockSpec((COMB_TM, LANES), row),
                  pl.BlockSpec((1, D_MODEL), lambda i: (0, 0)), pl.BlockSpec(memory_space=pl.ANY)],
        out_specs=pl.BlockSpec((COMB_TM, D_MODEL), row),
        out_shape=jax.ShapeDtypeStruct((t, D_MODEL), F32),
        scratch_shapes=[pltpu.VMEM((COMB_TM, D_MODEL), F32), pltpu.VMEM((COMB_TM, D_MODEL), F32),
                        pltpu.SemaphoreType.DMA(())],
        compiler_params=_cparams("arbitrary"),
        name="moe_combine",
    )(d1, d2, x1, route, g_final, ys)


def _hier_moe_tail(x1, xs, route, route_t, w1, w3, w2, g_final):
    t = x1.shape[0]
    dest, counts = _moe_plan(route_t)
    counts = counts[:N_EXPERTS, 0].astype(I32)
    padded = (counts + DISPATCH_BLOCK - 1) // DISPATCH_BLOCK * DISPATCH_BLOCK
    ends = jnp.cumsum(padded)
    n_blocks = (t * 2) // DISPATCH_BLOCK + N_EXPERTS
    block_start = jnp.arange(n_blocks, dtype=I32) * DISPATCH_BLOCK
    block_expert = jnp.minimum(jnp.sum((ends[None, :] <= block_start[:, None]).astype(I32), axis=1), N_EXPERTS - 1)
    n_used = (ends[-1:] // DISPATCH_BLOCK).astype(I32)
    d1, d2 = dest[0], dest[1]
    xs_sorted = _moe_dispatch(d1, d2, xs, n_blocks * DISPATCH_BLOCK)
    ys = _moe_experts(block_expert, n_used, xs_sorted, w1, w3, w2)
    return _moe_combine(d1, d2, x1, route, g_final, ys)


def kernel(x, g_mix, w_in, w_proj_a, w_proj_b, w_out, g_ffn, w_group, b_group, w_expert, b_expert, w1, w3, w2, g_final):
    batch, seq, d = x.shape
    assert d == D_MODEL and g_mix.shape[0] == 1, "one layer of width D_MODEL"
    x2 = x.reshape(batch * seq, d)
    cos_t, sin_t = _rope_tables(seq)
    qa, ka, va, qb, kb, vb, qi, ki2, wi, sga, sgb, kmean = _in_proj(
        x2, g_mix[0][None, :], _arrange_w_in(w_in[0]), cos_t, sin_t, seq)
    selb = _moba_select(qa, kmean, batch, seq)
    oa = _moba_attn(qa, ka, va, selb, batch, seq)
    ob = _dsa_attend(qb, kb, vb, _dsa_select(qi, ki2, wi, batch, seq), batch, seq)
    wr_hi, wr_lo, br = _router_params(w_group[0], b_group[0], w_expert[0], b_expert[0])
    x1, xs, route, route_t = _mix_out(
        x2, oa, ob, sga, sgb, w_proj_a[0].astype(BF16), w_proj_b[0].astype(BF16), w_out[0].astype(BF16),
        g_ffn[0][None, :], wr_hi, wr_lo, br)
    out = _hier_moe_tail(x1, xs, route, route_t, w1[0].astype(BF16), w3[0].astype(BF16), w2[0].astype(BF16),
                         g_final[None, :])
    return out.reshape(batch, seq, d)


def _router_params(w_group, b_group, w_expert, b_expert):
    w = jnp.pad(jnp.concatenate([w_group, w_expert], axis=1), ((0, 0), (0, LANES - N_GROUPS - N_EXPERTS)))
    b = jnp.pad(jnp.concatenate([b_group, b_expert]), (0, LANES - N_GROUPS - N_EXPERTS))[None, :]
    w_hi = w.astype(BF16)
    w_lo = (w - w_hi.astype(F32)).astype(BF16)
    return w_hi, w_lo, b
```

```python
import functools

import jax
import jax.numpy as jnp
from jax import lax
from jax.experimental import pallas as pl
from jax.experimental.pallas import tpu as pltpu

F32 = jnp.float32
BF16 = jnp.bfloat16
I32 = jnp.int32

D_MODEL = 1024
HEAD_DIM = 64
N_HEADS = 8
WIDTH = N_HEADS * HEAD_DIM
N_PAIRS = N_HEADS // 2
MOBA_BLOCK = 256
MOBA_TOPK = 3
DSA_TOPK = 256
IDX_SCALE = float(WIDTH) ** -0.5
ATTN_SCALE = float(HEAD_DIM) ** -0.5
N_GROUPS = 4
EXPERTS_PER_GROUP = 8
N_EXPERTS = N_GROUPS * EXPERTS_PER_GROUP
EXPERT_FF = 512
DISPATCH_BLOCK = 256
ROPE_THETA = 10000.0
RMS_EPS = 1e-6
NEG = -1e30

LANES = 128
VMEM_LIMIT = 56 * 1024 * 1024

C_QA, C_KA, C_VA, C_QB, C_KB, C_VB, C_QI = (i * WIDTH for i in range(7))
C_KI = 7 * WIDTH
C_WI = C_KI + LANES
C_GA = C_WI + WIDTH
C_GB = C_GA + D_MODEL
IN_COLS_PADDED = C_GB + D_MODEL


def _cparams(*semantics):
    return pltpu.CompilerParams(dimension_semantics=semantics, vmem_limit_bytes=VMEM_LIMIT)


def _dot_nt(a, b):
    return lax.dot_general(a, b, (((1,), (1,)), ((), ())), preferred_element_type=F32)


def _dot(a, b):
    return jnp.dot(a, b, preferred_element_type=F32)


IN_TM = 512


def _in_proj_kernel(x_ref, g_ref, w_ref, cos_ref, sin_ref,
                    qa_ref, ka_ref, va_ref, qb_ref, kb_ref, vb_ref, qi_ref, ki_ref, wi_ref,
                    sga_ref, sgb_ref, kmean_ref):
    x = x_ref[...]
    h = x * lax.rsqrt(jnp.mean(x * x, axis=-1, keepdims=True) + RMS_EPS) * g_ref[...]
    hb = h.astype(BF16)
    cos = cos_ref[...]
    sin = sin_ref[...]
    upper_half = (lax.broadcasted_iota(I32, (IN_TM, LANES), 1) & (HEAD_DIM // 2)) != 0

    def rope(v):
        partner = jnp.where(upper_half, pltpu.roll(v, HEAD_DIM // 2, 1), pltpu.roll(v, LANES - HEAD_DIM // 2, 1))
        return v * cos + partner * sin

    def proj(c0, width):
        return _dot(hb, w_ref[:, c0:c0 + width])

    def store_heads(ref, c0, rotary):
        r = proj(c0, WIDTH)
        for j in range(WIDTH // LANES):
            v = r[:, j * LANES:(j + 1) * LANES]
            ref[:, j * LANES:(j + 1) * LANES] = (rope(v) if rotary else v).astype(ref.dtype)

    store_heads(qa_ref, C_QA, True)
    store_heads(va_ref, C_VA, False)
    store_heads(qb_ref, C_QB, True)
    store_heads(kb_ref, C_KB, True)
    store_heads(vb_ref, C_VB, False)
    store_heads(qi_ref, C_QI, True)

    r = proj(C_KA, WIDTH)
    for j in range(WIDTH // LANES):
        v = rope(r[:, j * LANES:(j + 1) * LANES])
        ka_ref[:, j * LANES:(j + 1) * LANES] = v.astype(BF16)
        for blk in range(IN_TM // MOBA_BLOCK):
            kmean_ref[blk, :, j * LANES:(j + 1) * LANES] = jnp.mean(
                v[blk * MOBA_BLOCK:(blk + 1) * MOBA_BLOCK], axis=0, keepdims=True)

    ki_ref[...] = rope(proj(C_KI, LANES)).astype(BF16)
    wi_ref[...] = proj(C_WI, WIDTH) * IDX_SCALE
    sga_ref[...] = (1.0 / (1.0 + jnp.exp(-proj(C_GA, D_MODEL)))).astype(BF16)
    sgb_ref[...] = (1.0 / (1.0 + jnp.exp(-proj(C_GB, D_MODEL)))).astype(BF16)


def _in_proj(x2, g_mix, w_r, cos_t, sin_t, seq):
    t = x2.shape[0]
    n_tiles = t // IN_TM
    tiles_per_seq = seq // IN_TM
    row = lambda i: (i, 0)
    const = lambda i: (0, 0)
    act = lambda w, dt: jax.ShapeDtypeStruct((t, w), dt)
    out_shape = ([act(WIDTH, BF16)] * 7 + [act(LANES, BF16), act(WIDTH, F32), act(D_MODEL, BF16), act(D_MODEL, BF16),
                                           jax.ShapeDtypeStruct((t // MOBA_BLOCK, 1, WIDTH), F32)])
    out_specs = ([pl.BlockSpec((IN_TM, WIDTH), row)] * 7
                 + [pl.BlockSpec((IN_TM, LANES), row), pl.BlockSpec((IN_TM, WIDTH), row),
                    pl.BlockSpec((IN_TM, D_MODEL), row), pl.BlockSpec((IN_TM, D_MODEL), row),
                    pl.BlockSpec((IN_TM // MOBA_BLOCK, 1, WIDTH), lambda i: (i, 0, 0))])
    return pl.pallas_call(
        _in_proj_kernel,
        grid=(n_tiles,),
        in_specs=[pl.BlockSpec((IN_TM, D_MODEL), row),
                  pl.BlockSpec((1, D_MODEL), const),
                  pl.BlockSpec((D_MODEL, IN_COLS_PADDED), const, pipeline_mode=pl.Buffered(1)),
                  pl.BlockSpec((IN_TM, LANES), lambda i: (i % tiles_per_seq, 0)),
                  pl.BlockSpec((IN_TM, LANES), lambda i: (i % tiles_per_seq, 0))],
        out_specs=out_specs,
        out_shape=out_shape,
        compiler_params=_cparams("parallel"),
        name="in_proj",
    )(x2, g_mix, w_r, cos_t, sin_t)


def _moba_select_kernel(qa_ref, km_ref, selb_ref):
    own = pl.program_id(1)
    n_blocks = km_ref.shape[1]
    km = km_ref[0]
    col_head = lax.broadcasted_iota(I32, (N_HEADS, WIDTH), 1) // HEAD_DIM
    head_mask = col_head == lax.broadcasted_iota(I32, (N_HEADS, WIDTH), 0)
    rows = [jnp.where(head_mask, km[n:n + 1, :], 0.0) for n in range(n_blocks)]
    rows.append(jnp.zeros((LANES - n_blocks * N_HEADS, WIDTH), F32))
    km_t = jnp.concatenate(rows, axis=0).astype(BF16)
    gate = _dot_nt(qa_ref[...], km_t)
    lane = lax.broadcasted_iota(I32, gate.shape, 1)
    past = (lane // N_HEADS) < own
    g = jnp.where(past, gate, NEG)
    rank = jnp.zeros(gate.shape, I32)
    for r in range(1, n_blocks):
        later = pltpu.roll(g, LANES - N_HEADS * r, 1)
        earlier = pltpu.roll(g, N_HEADS * r, 1)
        rank = rank + (later > g).astype(I32) + (earlier >= g).astype(I32)
    chosen = jnp.where(past & (rank < MOBA_TOPK), 1.0, 0.0).astype(BF16)
    src = lax.broadcasted_iota(I32, (LANES, N_PAIRS * LANES), 0)
    dst = lax.broadcasted_iota(I32, (LANES, N_PAIRS * LANES), 1)
    src_n, src_h = src // N_HEADS, src % N_HEADS
    dst_pair, dst_w = dst // LANES, dst % LANES
    expand = ((src_n < n_blocks) & (dst_w < 2 * N_HEADS) & (dst_w % N_HEADS == src_n)
              & (dst_pair * 2 + dst_w // N_HEADS == src_h))
    hit = _dot(chosen, jnp.where(expand, 1.0, 0.0).astype(BF16))
    selb_ref[...] = jnp.where(hit > 0.5, 0.0, NEG)


def _moba_select(qa, kmean, batch, seq):
    n_blocks = seq // MOBA_BLOCK
    assert n_blocks * N_HEADS <= LANES and n_blocks <= N_HEADS
    t = batch * seq
    return pl.pallas_call(
        _moba_select_kernel,
        grid=(batch, n_blocks),
        in_specs=[pl.BlockSpec((MOBA_BLOCK, WIDTH), lambda b, i: (b * n_blocks + i, 0)),
                  pl.BlockSpec((1, n_blocks, WIDTH), lambda b, i: (b, 0, 0))],
        out_specs=pl.BlockSpec((MOBA_BLOCK, N_PAIRS * LANES), lambda b, i: (b * n_blocks + i, 0)),
        out_shape=jax.ShapeDtypeStruct((t, N_PAIRS * LANES), F32),
        compiler_params=_cparams("parallel", "parallel"),
        name="moba_select",
    )(qa, kmean.reshape(batch, n_blocks, WIDTH))


def _stack_heads(q2):
    lane = lax.broadcasted_iota(I32, q2.shape, 1)
    zero = jnp.zeros_like(q2)
    return jnp.concatenate([jnp.where(lane < HEAD_DIM, q2, zero), jnp.where(lane >= HEAD_DIM, q2, zero)], axis=0)


def _unstack_heads(o):
    rows = o.shape[0] // 2
    lane = lax.broadcasted_iota(I32, (rows, LANES), 1)
    return jnp.where(lane < HEAD_DIM, o[:rows], o[rows:])


def _masked_attention(s, v):
    p = jnp.exp(s - jnp.max(s, axis=-1, keepdims=True))
    return _dot(p.astype(BF16), v) / jnp.sum(p, axis=-1, keepdims=True)


PAIRS_PER_STEP = 2
STEP_LANES = PAIRS_PER_STEP * LANES


def _moba_attn_kernel(q_ref, k_ref, v_ref, selb_ref, o_ref):
    own = pl.program_id(2)
    n_blocks = k_ref.shape[0] // MOBA_BLOCK
    q_pos = lax.broadcasted_iota(I32, (2 * MOBA_BLOCK, MOBA_BLOCK), 0) % MOBA_BLOCK
    causal = lax.broadcasted_iota(I32, (2 * MOBA_BLOCK, MOBA_BLOCK), 1) <= q_pos
    for i in range(n_blocks):
        @pl.when(own == i)
        def _(i=i):
            n_keys = (i + 1) * MOBA_BLOCK
            for u in range(PAIRS_PER_STEP):
                pair = slice(u * LANES, (u + 1) * LANES)
                selb = selb_ref[:, pair]
                s = _dot_nt(_stack_heads(q_ref[:, pair]), k_ref[0:n_keys, pair])
                parts = []
                for j in range(i):
                    bias = jnp.concatenate([selb[:, j:j + 1], selb[:, N_HEADS + j:N_HEADS + j + 1]], axis=0)
                    parts.append(s[:, j * MOBA_BLOCK:(j + 1) * MOBA_BLOCK] + bias)
                parts.append(jnp.where(causal, s[:, i * MOBA_BLOCK:], NEG))
                o = _masked_attention(jnp.concatenate(parts, axis=1), v_ref[0:n_keys, pair])
                o_ref[:, pair] = _unstack_heads(o).astype(o_ref.dtype)


def _moba_attn(qa, ka, va, selb, batch, seq):
    n_blocks = seq // MOBA_BLOCK
    t = batch * seq
    q_map = lambda b, p, i: (b * n_blocks + i, p)
    kv_map = lambda b, p, i: (b, p)
    return pl.pallas_call(
        _moba_attn_kernel,
        grid=(batch, N_PAIRS // PAIRS_PER_STEP, n_blocks),
        in_specs=[pl.BlockSpec((MOBA_BLOCK, STEP_LANES), q_map),
                  pl.BlockSpec((seq, STEP_LANES), kv_map),
                  pl.BlockSpec((seq, STEP_LANES), kv_map),
                  pl.BlockSpec((MOBA_BLOCK, STEP_LANES), q_map)],
        out_specs=pl.BlockSpec((MOBA_BLOCK, STEP_LANES), q_map),
        out_shape=jax.ShapeDtypeStruct((t, WIDTH), BF16),
        compiler_params=_cparams("parallel", "parallel", "arbitrary"),
        name="moba_attn",
    )(qa, ka, va, selb)


def _rope_tables(seq):
    inv = jnp.power(ROPE_THETA, -jnp.arange(0, HEAD_DIM, 2, dtype=F32) / HEAD_DIM)
    ang = jnp.arange(seq, dtype=F32)[:, None] * inv[None, :]
    cos, sin = jnp.cos(ang), jnp.sin(ang)
    reps = LANES // HEAD_DIM
    return jnp.tile(jnp.concatenate([cos, cos], axis=-1), (1, reps)), jnp.tile(jnp.concatenate([-sin, sin], axis=-1), (1, reps))


def _arrange_w_in(w_in):
    sizes = (WIDTH,) * 7 + (HEAD_DIM, N_HEADS, D_MODEL, D_MODEL)
    offs = [0]
    for s in sizes:
        offs.append(offs[-1] + s)
    seg = [w_in[:, offs[i]:offs[i + 1]] for i in range(len(sizes))]
    qa, ka, va, qb, kb, vb, qi, ki, wi, ga, gb = seg
    wi_pairs = jnp.pad(wi.reshape(-1, N_PAIRS, 2), ((0, 0), (0, 0), (0, LANES - 2))).reshape(-1, WIDTH)
    w = jnp.concatenate([qa * ATTN_SCALE, ka, va, qb * ATTN_SCALE, kb, vb, qi, ki, ki, wi_pairs, ga, gb], axis=1)
    return w.astype(BF16)


DSA_TQ = 256
DSA_ROWS = 64
INT_MIN = -(2 ** 31)


def _count_lanes(hit):
    acc = hit[:, :LANES]
    for j in range(1, hit.shape[1] // LANES):
        acc = acc + hit[:, j * LANES:(j + 1) * LANES]
    return jnp.sum(acc, axis=-1, keepdims=True)


def _dsa_score_pair(i, qi_ref, ki_ref, wi_ref, score_ref):
    n_keys = (i + 1) * DSA_TQ
    logit = _dot_nt(_stack_heads(qi_ref[...]), ki_ref[0:n_keys, :])
    wi = wi_ref[...]
    part = wi[:, 0:1] * jnp.maximum(logit[:DSA_TQ], 0.0) + wi[:, 1:2] * jnp.maximum(logit[DSA_TQ:], 0.0)
    pair = pl.program_id(2)

    @pl.when(pair == 0)
    def _():
        score_ref[:, 0:n_keys] = part

    @pl.when(pair > 0)
    def _():
        score_ref[:, 0:n_keys] += part


def _dsa_thresholds(jobs):
    def as_float(code):
        return pltpu.bitcast(code ^ ((code >> 31) & 0x7FFFFFFF), F32)

    n_rg = DSA_TQ // DSA_ROWS
    units = [(i, score_ref, rg) for i, score_ref in jobs for rg in range(n_rg)]

    def step(unit, t_old, cand):
        i, score_ref, rg = unit
        k = score_ref[rg * DSA_ROWS:(rg + 1) * DSA_ROWS, 0:(i + 1) * DSA_TQ]
        n = _count_lanes(jnp.where(k >= as_float(cand), 1.0, 0.0))
        return jnp.where(n >= DSA_TOPK, cand, t_old)

    def bit_body(ib, ts):
        bit = jnp.left_shift(jnp.int32(1), 30 - ib)
        return tuple(step(u, t, t | bit) for u, t in zip(units, ts))

    ts = tuple(step(u, jnp.full((DSA_ROWS, 1), INT_MIN, I32), jnp.zeros((DSA_ROWS, 1), I32)) for u in units)
    ts = lax.fori_loop(0, 31, bit_body, ts)
    out = []
    for j in range(len(jobs)):
        code = jnp.concatenate(ts[j * n_rg:(j + 1) * n_rg], axis=0)
        out.append(jnp.where(code == INT_MIN, -jnp.inf, as_float(code)))
    return out


def _dsa_write_bias(i, score_ref, thr, bias_ref, first_row):
    n_keys = (i + 1) * DSA_TQ
    rows = slice(first_row, first_row + DSA_TQ)
    row = lax.broadcasted_iota(I32, (DSA_TQ, DSA_TQ), 0)
    col = lax.broadcasted_iota(I32, (DSA_TQ, DSA_TQ), 1)
    keys = score_ref[:, 0:n_keys]
    need = float(DSA_TOPK) - _count_lanes(jnp.where(keys > thr, 1.0, 0.0))
    strictly_before = jnp.where(row < col, 1.0, 0.0).astype(BF16)
    ties_seen = jnp.zeros((DSA_TQ, 1), F32)
    for t in range(i + 1):
        k = keys[:, t * DSA_TQ:(t + 1) * DSA_TQ]
        tie = jnp.where(k == thr, 1.0, 0.0)
        ties_before = ties_seen + _dot(tie.astype(BF16), strictly_before)
        chosen = (k > thr) | ((k == thr) & (ties_before < need))
        if t == i:
            chosen = chosen & (col <= row)
        bias_ref[rows, t * DSA_TQ:(t + 1) * DSA_TQ] = jnp.where(chosen, 0.0, NEG).astype(bias_ref.dtype)
        ties_seen = ties_seen + jnp.sum(tie, axis=-1, keepdims=True)
    if n_keys < bias_ref.shape[1]:
        bias_ref[rows, n_keys:] = jnp.full((DSA_TQ, bias_ref.shape[1] - n_keys), NEG, bias_ref.dtype)


def _dsa_select_kernel(qi_lo_ref, qi_hi_ref, ki_ref, wi_lo_ref, wi_hi_ref, bias_ref, score_lo_ref, score_hi_ref):
    n_chunks = ki_ref.shape[0] // DSA_TQ
    for j in range(n_chunks // 2):
        @pl.when(pl.program_id(1) == j)
        def _(j=j):
            lo, hi = j, n_chunks - 1 - j
            _dsa_score_pair(lo, qi_lo_ref, ki_ref, wi_lo_ref, score_lo_ref)
            _dsa_score_pair(hi, qi_hi_ref, ki_ref, wi_hi_ref, score_hi_ref)

            @pl.when(pl.program_id(2) == N_PAIRS - 1)
            def _():
                row = lax.broadcasted_iota(I32, (DSA_TQ, DSA_TQ), 0)
                col = lax.broadcasted_iota(I32, (DSA_TQ, DSA_TQ), 1)
                jobs = [(lo, score_lo_ref), (hi, score_hi_ref)]
                for i, score_ref in jobs:
                    diag = slice(i * DSA_TQ, (i + 1) * DSA_TQ)
                    score_ref[:, diag] = jnp.where(col <= row, score_ref[:, diag], NEG)
                for k, ((i, score_ref), thr) in enumerate(zip(jobs, _dsa_thresholds(jobs))):
                    _dsa_write_bias(i, score_ref, thr, bias_ref, k * DSA_TQ)


def _dsa_bias_block(c, n_chunks):
    return jnp.where(c < n_chunks // 2, 2 * c, 2 * (n_chunks - 1 - c) + 1)


def _dsa_select(qi, ki2, wi, batch, seq):
    assert seq % (2 * DSA_TQ) == 0 and min(DSA_TOPK, seq // 4) == DSA_TOPK
    n_chunks = seq // DSA_TQ
    lo_map = lambda b, j, p: (b * n_chunks + j, p)
    hi_map = lambda b, j, p: (b * n_chunks + n_chunks - 1 - j, p)
    return pl.pallas_call(
        _dsa_select_kernel,
        grid=(batch, n_chunks // 2, N_PAIRS),
        in_specs=[pl.BlockSpec((DSA_TQ, LANES), lo_map), pl.BlockSpec((DSA_TQ, LANES), hi_map),
                  pl.BlockSpec((seq, LANES), lambda b, j, p: (b, 0)),
                  pl.BlockSpec((DSA_TQ, LANES), lo_map), pl.BlockSpec((DSA_TQ, LANES), hi_map)],
        out_specs=pl.BlockSpec((2 * DSA_TQ, seq), lambda b, j, p: (b * (n_chunks // 2) + j, 0)),
        out_shape=jax.ShapeDtypeStruct((batch * seq, seq), BF16),
        scratch_shapes=[pltpu.VMEM((DSA_TQ, seq), F32), pltpu.VMEM((DSA_TQ, seq), F32)],
        compiler_params=_cparams("parallel", "arbitrary", "arbitrary"),
        name="dsa_select",
    )(qi, qi, ki2, wi, wi)


def _dsa_attend_kernel(q_ref, k_ref, v_ref, bias_ref, o_ref):
    for i in range(k_ref.shape[0] // DSA_TQ):
        @pl.when(pl.program_id(1) == i)
        def _(i=i):
            n_keys = (i + 1) * DSA_TQ
            bias = bias_ref[:, 0:n_keys].astype(F32)
            bias = jnp.concatenate([bias, bias], axis=0)
            for u in range(PAIRS_PER_STEP):
                pair = slice(u * LANES, (u + 1) * LANES)
                s = _dot_nt(_stack_heads(q_ref[:, pair]), k_ref[0:n_keys, pair]) + bias
                o_ref[:, pair] = _unstack_heads(_masked_attention(s, v_ref[0:n_keys, pair])).astype(o_ref.dtype)


def _dsa_attend(qb, kb, vb, bias, batch, seq):
    n_chunks = seq // DSA_TQ
    q_map = lambda b, c, p: (b * n_chunks + c, p)
    kv_map = lambda b, c, p: (b, p)
    return pl.pallas_call(
        _dsa_attend_kernel,
        grid=(batch, n_chunks, N_PAIRS // PAIRS_PER_STEP),
        in_specs=[pl.BlockSpec((DSA_TQ, STEP_LANES), q_map),
                  pl.BlockSpec((seq, STEP_LANES), kv_map),
                  pl.BlockSpec((seq, STEP_LANES), kv_map),
                  pl.BlockSpec((DSA_TQ, seq), lambda b, c, p: (b * n_chunks + _dsa_bias_block(c, n_chunks), 0))],
        out_specs=pl.BlockSpec((DSA_TQ, STEP_LANES), q_map),
        out_shape=jax.ShapeDtypeStruct((batch * seq, WIDTH), BF16),
        compiler_params=_cparams("parallel", "parallel", "arbitrary"),
        name="dsa_attend",
    )(qb, kb, vb, bias)


MIX_TM = 512
ROUTE_ROWS = 8


def _mix_out_kernel(x_ref, oa_ref, ob_ref, sga_ref, sgb_ref, wa_ref, wb_ref, wo_ref, g_ref, wr_hi_ref, wr_lo_ref, br_ref,
                    x1_ref, xs_ref, route_ref, route_t_ref):
    mixed = (sga_ref[...].astype(F32) * _dot(oa_ref[...], wa_ref[...])
             + sgb_ref[...].astype(F32) * _dot(ob_ref[...], wb_ref[...]))
    x1 = x_ref[...] + _dot(mixed.astype(BF16), wo_ref[...])
    x1_ref[...] = x1
    hn = x1 * lax.rsqrt(jnp.mean(x1 * x1, axis=-1, keepdims=True) + RMS_EPS) * g_ref[...]
    xs_ref[...] = hn

    hi = hn.astype(BF16)
    lo = (hn - hi.astype(F32)).astype(BF16)
    logits = _dot(hi, wr_hi_ref[...]) + _dot(lo, wr_hi_ref[...]) + _dot(hi, wr_lo_ref[...]) + br_ref[...]
    lane = lax.broadcasted_iota(I32, logits.shape, 1)
    far = jnp.int32(LANES)

    def first_lane_of_max(v, valid):
        top = jnp.max(jnp.where(valid, v, NEG), axis=-1, keepdims=True)
        return top, jnp.min(jnp.where(valid & (v == top), lane, far), axis=-1, keepdims=True)

    is_group = lane < N_GROUPS
    g_max, g_sel = first_lane_of_max(logits, is_group)
    g_w = 1.0 / jnp.sum(jnp.where(is_group, jnp.exp(logits - g_max), 0.0), axis=-1, keepdims=True)
    first = N_GROUPS + g_sel * EXPERTS_PER_GROUP
    in_group = (lane >= first) & (lane < first + EXPERTS_PER_GROUP)
    e_max, _ = first_lane_of_max(logits, in_group)
    e_exp = jnp.where(in_group, jnp.exp(logits - e_max), 0.0)
    prob = e_exp / jnp.sum(e_exp, axis=-1, keepdims=True)
    p1, i1 = first_lane_of_max(prob, in_group)
    p2, i2 = first_lane_of_max(prob, in_group & (lane != i1))
    denom = p1 + p2
    record = jnp.where(lane == 0, (i1 - N_GROUPS).astype(F32),
                       jnp.where(lane == 1, (i2 - N_GROUPS).astype(F32),
                                 jnp.where(lane == 2, g_w * p1 / denom,
                                           jnp.where(lane == 3, g_w * p2 / denom, 0.0))))
    route_ref[...] = record
    route_t_ref[...] = record.T[:ROUTE_ROWS, :]


def _mix_out(x2, oa, ob, sga, sgb, wa, wb, wo, g_ffn, wr_hi, wr_lo, br):
    t = x2.shape[0]
    row = lambda i: (i, 0)
    const = lambda i: (0, 0)
    once = dict(pipeline_mode=pl.Buffered(1))
    return pl.pallas_call(
        _mix_out_kernel,
        grid=(t // MIX_TM,),
        in_specs=[pl.BlockSpec((MIX_TM, D_MODEL), row),
                  pl.BlockSpec((MIX_TM, WIDTH), row), pl.BlockSpec((MIX_TM, WIDTH), row),
                  pl.BlockSpec((MIX_TM, D_MODEL), row), pl.BlockSpec((MIX_TM, D_MODEL), row),
                  pl.BlockSpec((WIDTH, D_MODEL), const, **once), pl.BlockSpec((WIDTH, D_MODEL), const, **once),
                  pl.BlockSpec((D_MODEL, D_MODEL), const, **once), pl.BlockSpec((1, D_MODEL), const),
                  pl.BlockSpec((D_MODEL, LANES), const, **once), pl.BlockSpec((D_MODEL, LANES), const, **once),
                  pl.BlockSpec((1, LANES), const)],
        out_specs=[pl.BlockSpec((MIX_TM, D_MODEL), row), pl.BlockSpec((MIX_TM, D_MODEL), row),
                   pl.BlockSpec((MIX_TM, LANES), row), pl.BlockSpec((ROUTE_ROWS, MIX_TM), lambda i: (0, i))],
        out_shape=[jax.ShapeDtypeStruct((t, D_MODEL), F32), jax.ShapeDtypeStruct((t, D_MODEL), F32),
                   jax.ShapeDtypeStruct((t, LANES), F32), jax.ShapeDtypeStruct((ROUTE_ROWS, t), F32)],
        compiler_params=_cparams("parallel"),
        name="mix_out",
    )(x2, oa, ob, sga, sgb, wa, wb, wo, g_ffn, wr_hi, wr_lo, br)


PLAN_TM = 512


def _moe_plan_kernel(route_t_ref, dest_ref, counts_ref, count_ref, start_ref):
    phase = pl.program_id(0)
    step = pl.program_id(1)
    expert = lax.broadcasted_iota(I32, (LANES, PLAN_TM), 0)
    e1 = route_t_ref[0:1, :].astype(I32)
    e2 = route_t_ref[1:2, :].astype(I32)
    hot1 = expert == e1
    hot2 = expert == e2
    hot = jnp.where(hot1 | hot2, 1.0, 0.0)

    @pl.when((phase == 0) & (step == 0))
    def _():
        count_ref[...] = jnp.zeros_like(count_ref)

    @pl.when(phase == 0)
    def _():
        count_ref[...] += jnp.sum(hot, axis=-1, keepdims=True)
        dest_ref[...] = jnp.zeros_like(dest_ref)

    @pl.when((phase == 1) & (step == 0))
    def _():
        counts = jnp.broadcast_to(count_ref[...], (LANES, LANES))
        counts_ref[...] = counts
        padded = jnp.ceil(counts / DISPATCH_BLOCK) * DISPATCH_BLOCK
        sub = lax.broadcasted_iota(I32, (LANES, LANES), 0)
        ends = padded
        shift = 1
        while shift < LANES:
            ends = ends + jnp.where(sub >= shift, pltpu.roll(ends, shift, 0), 0.0)
            shift *= 2
        start_ref[...] = (ends - padded)[:, 0:1]
        count_ref[...] = jnp.zeros_like(count_ref)

    @pl.when(phase == 1)
    def _():
        tok_r = lax.broadcasted_iota(I32, (PLAN_TM, PLAN_TM), 0)
        tok_c = lax.broadcasted_iota(I32, (PLAN_TM, PLAN_TM), 1)
        earlier = jnp.where(tok_r < tok_c, 1.0, 0.0).astype(BF16)
        slot = start_ref[...] + count_ref[...] + _dot(hot.astype(BF16), earlier)
        d1 = jnp.sum(jnp.where(hot1, slot, 0.0), axis=0, keepdims=True)
        d2 = jnp.sum(jnp.where(hot2, slot, 0.0), axis=0, keepdims=True)
        sub = lax.broadcasted_iota(I32, (ROUTE_ROWS, PLAN_TM), 0)
        dest_ref[...] = jnp.where(sub == 0, d1, jnp.where(sub == 1, d2, 0.0)).astype(I32)
        count_ref[...] += jnp.sum(hot, axis=-1, keepdims=True)


def _moe_plan(route_t):
    t = route_t.shape[1]
    return pl.pallas_call(
        _moe_plan_kernel,
        grid=(2, t // PLAN_TM),
        in_specs=[pl.BlockSpec((ROUTE_ROWS, PLAN_TM), lambda ph, i: (0, i))],
        out_specs=[pl.BlockSpec((ROUTE_ROWS, PLAN_TM), lambda ph, i: (0, i * ph)),
                   pl.BlockSpec((LANES, LANES), lambda ph, i: (0, 0))],
        out_shape=[jax.ShapeDtypeStruct((ROUTE_ROWS, t), I32), jax.ShapeDtypeStruct((LANES, LANES), F32)],
        scratch_shapes=[pltpu.VMEM((LANES, 1), F32), pltpu.VMEM((LANES, 1), F32)],
        compiler_params=_cparams("arbitrary", "arbitrary"),
        name="moe_plan",
    )(route_t)


DISP_TM = 1024
DMA_UNROLL = 8


def _moe_dispatch_kernel(d1_ref, d2_ref, xs_ref, zeros_ref, out_ref, sem):
    del zeros_ref

    def row_copy(r, dest):
        return pltpu.make_async_copy(xs_ref.at[pl.ds(r, 1), :], out_ref.at[pl.ds(dest, 1), :], sem)

    def issue(r, carry):
        row_copy(r, d1_ref[r]).start(priority=0)
        row_copy(r, d2_ref[r]).start(priority=1)
        return carry
    lax.fori_loop(0, DISP_TM, issue, 0, unroll=DMA_UNROLL)

    for _ in range(2):
        pltpu.make_async_copy(xs_ref, out_ref.at[pl.ds(0, DISP_TM), :], sem).wait()


def _moe_dispatch(d1, d2, xs, n_rows):
    t = xs.shape[0]
    smem = lambda: pl.BlockSpec((DISP_TM,), lambda i: (i,), memory_space=pltpu.SMEM)
    return pl.pallas_call(
        _moe_dispatch_kernel,
        grid=(t // DISP_TM,),
        in_specs=[smem(), smem(), pl.BlockSpec((DISP_TM, D_MODEL), lambda i: (i, 0)),
                  pl.BlockSpec(memory_space=pl.ANY)],
        out_specs=pl.BlockSpec(memory_space=pl.ANY),
        out_shape=jax.ShapeDtypeStruct((n_rows, D_MODEL), F32),
        scratch_shapes=[pltpu.SemaphoreType.DMA(())],
        input_output_aliases={3: 0},
        compiler_params=_cparams("arbitrary"),
        name="moe_dispatch",
    )(d1, d2, xs, jnp.zeros((n_rows, D_MODEL), F32))


def _moe_expert_kernel(block_expert_ref, n_used_ref, xs_ref, w1_ref, w3_ref, w2_ref, ys_ref):
    del block_expert_ref
    live = pl.program_id(0) < n_used_ref[0]

    @pl.when(live)
    def _():
        x = xs_ref[...].astype(BF16)
        h1 = _dot(x, w1_ref[0])
        h3 = _dot(x, w3_ref[0])
        hid = h1 / (1.0 + jnp.exp(-h1)) * h3
        ys_ref[...] = _dot(hid.astype(BF16), w2_ref[0])

    @pl.when(jnp.logical_not(live))
    def _():
        ys_ref[...] = jnp.zeros_like(ys_ref)


def _moe_experts(block_expert, n_used, xs_sorted, w1, w3, w2):
    n_rows = xs_sorted.shape[0]
    grid_spec = pltpu.PrefetchScalarGridSpec(
        num_scalar_prefetch=2,
        grid=(n_rows // DISPATCH_BLOCK,),
        in_specs=[pl.BlockSpec((DISPATCH_BLOCK, D_MODEL), lambda j, be, nu: (j, 0)),
                  pl.BlockSpec((1, D_MODEL, EXPERT_FF), lambda j, be, nu: (be[j], 0, 0)),
                  pl.BlockSpec((1, D_MODEL, EXPERT_FF), lambda j, be, nu: (be[j], 0, 0)),
                  pl.BlockSpec((1, EXPERT_FF, D_MODEL), lambda j, be, nu: (be[j], 0, 0))],
        out_specs=pl.BlockSpec((DISPATCH_BLOCK, D_MODEL), lambda j, be, nu: (j, 0)))
    return pl.pallas_call(
        _moe_expert_kernel,
        grid_spec=grid_spec,
        out_shape=jax.ShapeDtypeStruct((n_rows, D_MODEL), F32),
        compiler_params=_cparams("arbitrary"),
        name="moe_experts",
    )(block_expert, n_used, xs_sorted, w1, w3, w2)


COMB_TM = 512


def _moe_combine_kernel(d1_ref, d2_ref, x1_ref, route_ref, g_ref, ys_ref, out_ref, y1_ref, y2_ref, sem):
    def row_copy(src, r, buf):
        return pltpu.make_async_copy(ys_ref.at[pl.ds(src, 1), :], buf.at[pl.ds(r, 1), :], sem)

    def issue(r, carry):
        row_copy(d1_ref[r], r, y1_ref).start(priority=0)
        row_copy(d2_ref[r], r, y2_ref).start(priority=1)
        return carry
    lax.fori_loop(0, COMB_TM, issue, 0, unroll=DMA_UNROLL)

    for buf in (y1_ref, y2_ref):
        pltpu.make_async_copy(ys_ref.at[pl.ds(0, COMB_TM), :], buf, sem).wait()

    route = route_ref[...]
    x2 = x1_ref[...] + (route[:, 2:3] * y1_ref[...] + route[:, 3:4] * y2_ref[...])
    out_ref[...] = x2 * lax.rsqrt(jnp.mean(x2 * x2, axis=-1, keepdims=True) + RMS_EPS) * g_ref[...]


def _moe_combine(d1, d2, x1, route, g_final, ys):
    t = x1.shape[0]
    smem = lambda: pl.BlockSpec((COMB_TM,), lambda i: (i,), memory_space=pltpu.SMEM)
    row = lambda i: (i, 0)
    return pl.pallas_call(
        _moe_combine_kernel,
        grid=(t // COMB_TM,),
        in_specs=[smem(), smem(), pl.BlockSpec((COMB_TM, D_MODEL), row), pl.BlockSpec((COMB_TM, LANES), row),
                  pl.BlockSpec((1, D_MODEL), lambda i: (0, 0)), pl.BlockSpec(memory_space=pl.ANY)],
        out_specs=pl.BlockSpec((COMB_TM, D_MODEL), row),
        out_shape=jax.ShapeDtypeStruct((t, D_MODEL), F32),
        scratch_shapes=[pltpu.VMEM((COMB_TM, D_MODEL), F32), pltpu.VMEM((COMB_TM, D_MODEL), F32),
                        pltpu.SemaphoreType.DMA(())],
        compiler_params=_cparams("arbitrary"),
        name="moe_combine",
    )(d1, d2, x1, route, g_final, ys)


def _hier_moe_tail(x1, xs, route, route_t, w1, w3, w2, g_final):
    t = x1.shape[0]
    dest, counts = _moe_plan(route_t)
    counts = counts[:N_EXPERTS, 0].astype(I32)
    padded = (counts + DISPATCH_BLOCK - 1) // DISPATCH_BLOCK * DISPATCH_BLOCK
    ends = jnp.cumsum(padded)
    n_blocks = (t * 2) // DISPATCH_BLOCK + N_EXPERTS
    block_start = jnp.arange(n_blocks, dtype=I32) * DISPATCH_BLOCK
    block_expert = jnp.minimum(jnp.sum((ends[None, :] <= block_start[:, None]).astype(I32), axis=1), N_EXPERTS - 1)
    n_used = (ends[-1:] // DISPATCH_BLOCK).astype(I32)
    d1, d2 = dest[0], dest[1]
    xs_sorted = _moe_dispatch(d1, d2, xs, n_blocks * DISPATCH_BLOCK)
    ys = _moe_experts(block_expert, n_used, xs_sorted, w1, w3, w2)
    return _moe_combine(d1, d2, x1, route, g_final, ys)


def kernel(x, g_mix, w_in, w_proj_a, w_proj_b, w_out, g_ffn, w_group, b_group, w_expert, b_expert, w1, w3, w2, g_final):
    batch, seq, d = x.shape
    assert d == D_MODEL and g_mix.shape[0] == 1, "one layer of width D_MODEL"
    x2 = x.reshape(batch * seq, d)
    cos_t, sin_t = _rope_tables(seq)
    qa, ka, va, qb, kb, vb, qi, ki2, wi, sga, sgb, kmean = _in_proj(
        x2, g_mix[0][None, :], _arrange_w_in(w_in[0]), cos_t, sin_t, seq)
    selb = _moba_select(qa, kmean, batch, seq)
    oa = _moba_attn(qa, ka, va, selb, batch, seq)
    ob = _dsa_attend(qb, kb, vb, _dsa_select(qi, ki2, wi, batch, seq), batch, seq)
    wr_hi, wr_lo, br = _router_params(w_group[0], b_group[0], w_expert[0], b_expert[0])
    x1, xs, route, route_t = _mix_out(
        x2, oa, ob, sga, sgb, w_proj_a[0].astype(BF16), w_proj_b[0].astype(BF16), w_out[0].astype(BF16),
        g_ffn[0][None, :], wr_hi, wr_lo, br)
    out = _hier_moe_tail(x1, xs, route, route_t, w1[0].astype(BF16), w3[0].astype(BF16), w2[0].astype(BF16),
                         g_final[None, :])
    return out.reshape(batch, seq, d)


def _router_params(w_group, b_group, w_expert, b_expert):
    w = jnp.pad(jnp.concatenate([w_group, w_expert], axis=1), ((0, 0), (0, LANES - N_GROUPS - N_EXPERTS)))
    b = jnp.pad(jnp.concatenate([b_group, b_expert]), (0, LANES - N_GROUPS - N_EXPERTS))[None, :]
    w_hi = w.astype(BF16)
    w_lo = (w - w_hi.astype(F32)).astype(BF16)
    return w_hi, w_lo, b
```

```python
import functools

import jax
import jax.numpy as jnp
from jax import lax
from jax.experimental import pallas as pl
from jax.experimental.pallas import tpu as pltpu

F32 = jnp.float32
BF16 = jnp.bfloat16
I32 = jnp.int32

D_MODEL = 1024
HEAD_DIM = 64
N_HEADS = 8
WIDTH = N_HEADS * HEAD_DIM
N_PAIRS = N_HEADS // 2
MOBA_BLOCK = 256
MOBA_TOPK = 3
DSA_TOPK = 256
IDX_SCALE = float(WIDTH) ** -0.5
ATTN_SCALE = float(HEAD_DIM) ** -0.5
N_GROUPS = 4
EXPERTS_PER_GROUP = 8
N_EXPERTS = N_GROUPS * EXPERTS_PER_GROUP
EXPERT_FF = 512
DISPATCH_BLOCK = 512
ROPE_THETA = 10000.0
RMS_EPS = 1e-6
NEG = -1e30

LANES = 128
VMEM_LIMIT = 56 * 1024 * 1024

C_QA, C_KA, C_VA, C_QB, C_KB, C_VB, C_QI = (i * WIDTH for i in range(7))
C_KI = 7 * WIDTH
C_WI = C_KI + LANES
C_GA = C_WI + WIDTH
C_GB = C_GA + D_MODEL
IN_COLS_PADDED = C_GB + D_MODEL


def _cparams(*semantics):
    return pltpu.CompilerParams(dimension_semantics=semantics, vmem_limit_bytes=VMEM_LIMIT)


def _dot_nt(a, b):
    return lax.dot_general(a, b, (((1,), (1,)), ((), ())), preferred_element_type=F32)


def _dot(a, b):
    return jnp.dot(a, b, preferred_element_type=F32)


IN_TM = 512


def _in_proj_kernel(x_ref, g_ref, w_ref, cos_ref, sin_ref,
                    qa_ref, ka_ref, va_ref, qb_ref, kb_ref, vb_ref, qi_ref, ki_ref, wi_ref,
                    sga_ref, sgb_ref, kmean_ref):
    x = x_ref[...]
    h = x * lax.rsqrt(jnp.mean(x * x, axis=-1, keepdims=True) + RMS_EPS) * g_ref[...]
    hb = h.astype(BF16)
    cos = cos_ref[...]
    sin = sin_ref[...]
    upper_half = (lax.broadcasted_iota(I32, (IN_TM, LANES), 1) & (HEAD_DIM // 2)) != 0

    def rope(v):
        partner = jnp.where(upper_half, pltpu.roll(v, HEAD_DIM // 2, 1), pltpu.roll(v, LANES - HEAD_DIM // 2, 1))
        return v * cos + partner * sin

    def proj(c0, width):
        return _dot(hb, w_ref[:, c0:c0 + width])

    def store_heads(ref, c0, rotary):
        r = proj(c0, WIDTH)
        for j in range(WIDTH // LANES):
            v = r[:, j * LANES:(j + 1) * LANES]
            ref[:, j * LANES:(j + 1) * LANES] = (rope(v) if rotary else v).astype(ref.dtype)

    store_heads(qa_ref, C_QA, True)
    store_heads(va_ref, C_VA, False)
    store_heads(qb_ref, C_QB, True)
    store_heads(kb_ref, C_KB, True)
    store_heads(vb_ref, C_VB, False)
    store_heads(qi_ref, C_QI, True)

    r = proj(C_KA, WIDTH)
    for j in range(WIDTH // LANES):
        v = rope(r[:, j * LANES:(j + 1) * LANES])
        ka_ref[:, j * LANES:(j + 1) * LANES] = v.astype(BF16)
        for blk in range(IN_TM // MOBA_BLOCK):
            kmean_ref[blk, :, j * LANES:(j + 1) * LANES] = jnp.mean(
                v[blk * MOBA_BLOCK:(blk + 1) * MOBA_BLOCK], axis=0, keepdims=True)

    ki_ref[...] = rope(proj(C_KI, LANES)).astype(BF16)
    wi_ref[...] = proj(C_WI, WIDTH) * IDX_SCALE
    sga_ref[...] = (1.0 / (1.0 + jnp.exp(-proj(C_GA, D_MODEL)))).astype(BF16)
    sgb_ref[...] = (1.0 / (1.0 + jnp.exp(-proj(C_GB, D_MODEL)))).astype(BF16)


def _in_proj(x2, g_mix, w_r, cos_t, sin_t, seq):
    t = x2.shape[0]
    n_tiles = t // IN_TM
    tiles_per_seq = seq // IN_TM
    row = lambda i: (i, 0)
    const = lambda i: (0, 0)
    act = lambda w, dt: jax.ShapeDtypeStruct((t, w), dt)
    out_shape = ([act(WIDTH, BF16)] * 7 + [act(LANES, BF16), act(WIDTH, F32), act(D_MODEL, BF16), act(D_MODEL, BF16),
                                           jax.ShapeDtypeStruct((t // MOBA_BLOCK, 1, WIDTH), F32)])
    out_specs = ([pl.BlockSpec((IN_TM, WIDTH), row)] * 7
                 + [pl.BlockSpec((IN_TM, LANES), row), pl.BlockSpec((IN_TM, WIDTH), row),
                    pl.BlockSpec((IN_TM, D_MODEL), row), pl.BlockSpec((IN_TM, D_MODEL), row),
                    pl.BlockSpec((IN_TM // MOBA_BLOCK, 1, WIDTH), lambda i: (i, 0, 0))])
    return pl.pallas_call(
        _in_proj_kernel,
        grid=(n_tiles,),
        in_specs=[pl.BlockSpec((IN_TM, D_MODEL), row),
                  pl.BlockSpec((1, D_MODEL), const),
                  pl.BlockSpec((D_MODEL, IN_COLS_PADDED), const, pipeline_mode=pl.Buffered(1)),
                  pl.BlockSpec((IN_TM, LANES), lambda i: (i % tiles_per_seq, 0)),
                  pl.BlockSpec((IN_TM, LANES), lambda i: (i % tiles_per_seq, 0))],
        out_specs=out_specs,
        out_shape=out_shape,
        compiler_params=_cparams("parallel"),
        name="in_proj",
    )(x2, g_mix, w_r, cos_t, sin_t)


def _moba_select_kernel(qa_ref, km_ref, selb_ref):
    own = pl.program_id(1)
    n_blocks = km_ref.shape[1]
    km = km_ref[0]
    col_head = lax.broadcasted_iota(I32, (N_HEADS, WIDTH), 1) // HEAD_DIM
    head_mask = col_head == lax.broadcasted_iota(I32, (N_HEADS, WIDTH), 0)
    rows = [jnp.where(head_mask, km[n:n + 1, :], 0.0) for n in range(n_blocks)]
    rows.append(jnp.zeros((LANES - n_blocks * N_HEADS, WIDTH), F32))
    km_t = jnp.concatenate(rows, axis=0).astype(BF16)
    gate = _dot_nt(qa_ref[...], km_t)
    lane = lax.broadcasted_iota(I32, gate.shape, 1)
    past = (lane // N_HEADS) < own
    g = jnp.where(past, gate, NEG)
    rank = jnp.zeros(gate.shape, I32)
    for r in range(1, n_blocks):
        later = pltpu.roll(g, LANES - N_HEADS * r, 1)
        earlier = pltpu.roll(g, N_HEADS * r, 1)
        rank = rank + (later > g).astype(I32) + (earlier >= g).astype(I32)
    chosen = jnp.where(past & (rank < MOBA_TOPK), 1.0, 0.0).astype(BF16)
    src = lax.broadcasted_iota(I32, (LANES, N_PAIRS * LANES), 0)
    dst = lax.broadcasted_iota(I32, (LANES, N_PAIRS * LANES), 1)
    src_n, src_h = src // N_HEADS, src % N_HEADS
    dst_pair, dst_w = dst // LANES, dst % LANES
    expand = ((src_n < n_blocks) & (dst_w < 2 * N_HEADS) & (dst_w % N_HEADS == src_n)
              & (dst_pair * 2 + dst_w // N_HEADS == src_h))
    hit = _dot(chosen, jnp.where(expand, 1.0, 0.0).astype(BF16))
    selb_ref[...] = jnp.where(hit > 0.5, 0.0, NEG).astype(selb_ref.dtype)


def _moba_select(qa, kmean, batch, seq):
    n_blocks = seq // MOBA_BLOCK
    assert n_blocks * N_HEADS <= LANES and n_blocks <= N_HEADS
    t = batch * seq
    return pl.pallas_call(
        _moba_select_kernel,
        grid=(batch, n_blocks),
        in_specs=[pl.BlockSpec((MOBA_BLOCK, WIDTH), lambda b, i: (b * n_blocks + i, 0)),
                  pl.BlockSpec((1, n_blocks, WIDTH), lambda b, i: (b, 0, 0))],
        out_specs=pl.BlockSpec((MOBA_BLOCK, N_PAIRS * LANES), lambda b, i: (b * n_blocks + i, 0)),
        out_shape=jax.ShapeDtypeStruct((t, N_PAIRS * LANES), BF16),
        compiler_params=_cparams("parallel", "parallel"),
        name="moba_select",
    )(qa, kmean.reshape(batch, n_blocks, WIDTH))


def _stack_heads(q2):
    lane = lax.broadcasted_iota(I32, q2.shape, 1)
    zero = jnp.zeros_like(q2)
    return jnp.concatenate([jnp.where(lane < HEAD_DIM, q2, zero), jnp.where(lane >= HEAD_DIM, q2, zero)], axis=0)


def _unstack_heads(o):
    rows = o.shape[0] // 2
    lane = lax.broadcasted_iota(I32, (rows, LANES), 1)
    return jnp.where(lane < HEAD_DIM, o[:rows], o[rows:])


def _masked_attention(s, v):
    p = jnp.exp(s - jnp.max(s, axis=-1, keepdims=True))
    return _dot(p.astype(BF16), v) / jnp.sum(p, axis=-1, keepdims=True)


PAIRS_PER_STEP = 2
STEP_LANES = PAIRS_PER_STEP * LANES


def _moba_attn_kernel(q_ref, k_ref, v_ref, selb_ref, o_ref):
    own = pl.program_id(2)
    n_blocks = k_ref.shape[0] // MOBA_BLOCK
    q_pos = lax.broadcasted_iota(I32, (2 * MOBA_BLOCK, MOBA_BLOCK), 0) % MOBA_BLOCK
    causal = lax.broadcasted_iota(I32, (2 * MOBA_BLOCK, MOBA_BLOCK), 1) <= q_pos
    for i in range(n_blocks):
        @pl.when(own == i)
        def _(i=i):
            n_keys = (i + 1) * MOBA_BLOCK
            key_blk = lax.broadcasted_iota(I32, (n_keys, LANES), 0) // MOBA_BLOCK
            key_lane = lax.broadcasted_iota(I32, (n_keys, LANES), 1)
            indicator = jnp.where((key_lane < 2 * N_HEADS) & (key_lane % N_HEADS == key_blk) & (key_blk < i),
                                  1.0, 0.0).astype(BF16)
            q_lane = lax.broadcasted_iota(I32, (MOBA_BLOCK, LANES), 1)
            for u in range(PAIRS_PER_STEP):
                pair = slice(u * LANES, (u + 1) * LANES)
                selb = selb_ref[:, pair]
                zero = jnp.zeros_like(selb)
                bias_rows = jnp.concatenate([jnp.where(q_lane < N_HEADS, selb, zero),
                                             jnp.where((q_lane >= N_HEADS) & (q_lane < 2 * N_HEADS), selb, zero)], axis=0)
                s = _dot_nt(jnp.concatenate([_stack_heads(q_ref[:, pair]), bias_rows], axis=1),
                            jnp.concatenate([k_ref[0:n_keys, pair], indicator], axis=1))
                own_s = jnp.where(causal, s[:, i * MOBA_BLOCK:], NEG)
                s = own_s if i == 0 else jnp.concatenate([s[:, :i * MOBA_BLOCK], own_s], axis=1)
                o_ref[:, pair] = _unstack_heads(_masked_attention(s, v_ref[0:n_keys, pair])).astype(o_ref.dtype)


def _moba_attn(qa, ka, va, selb, batch, seq):
    n_blocks = seq // MOBA_BLOCK
    t = batch * seq
    q_map = lambda b, p, i: (b * n_blocks + i, p)
    kv_map = lambda b, p, i: (b, p)
    return pl.pallas_call(
        _moba_attn_kernel,
        grid=(batch, N_PAIRS // PAIRS_PER_STEP, n_blocks),
        in_specs=[pl.BlockSpec((MOBA_BLOCK, STEP_LANES), q_map),
                  pl.BlockSpec((seq, STEP_LANES), kv_map),
                  pl.BlockSpec((seq, STEP_LANES), kv_map),
                  pl.BlockSpec((MOBA_BLOCK, STEP_LANES), q_map)],
        out_specs=pl.BlockSpec((MOBA_BLOCK, STEP_LANES), q_map),
        out_shape=jax.ShapeDtypeStruct((t, WIDTH), BF16),
        compiler_params=_cparams("parallel", "parallel", "arbitrary"),
        name="moba_attn",
    )(qa, ka, va, selb)


def _rope_tables(seq):
    inv = jnp.power(ROPE_THETA, -jnp.arange(0, HEAD_DIM, 2, dtype=F32) / HEAD_DIM)
    ang = jnp.arange(seq, dtype=F32)[:, None] * inv[None, :]
    cos, sin = jnp.cos(ang), jnp.sin(ang)
    reps = LANES // HEAD_DIM
    return jnp.tile(jnp.concatenate([cos, cos], axis=-1), (1, reps)), jnp.tile(jnp.concatenate([-sin, sin], axis=-1), (1, reps))


def _arrange_w_in(w_in):
    sizes = (WIDTH,) * 7 + (HEAD_DIM, N_HEADS, D_MODEL, D_MODEL)
    offs = [0]
    for s in sizes:
        offs.append(offs[-1] + s)
    seg = [w_in[:, offs[i]:offs[i + 1]] for i in range(len(sizes))]
    qa, ka, va, qb, kb, vb, qi, ki, wi, ga, gb = seg
    wi_pairs = jnp.pad(wi.reshape(-1, N_PAIRS, 2), ((0, 0), (0, 0), (0, LANES - 2))).reshape(-1, WIDTH)
    w = jnp.concatenate([qa * ATTN_SCALE, ka, va, qb * ATTN_SCALE, kb, vb, qi, ki, ki, wi_pairs, ga, gb], axis=1)
    return w.astype(BF16)


DSA_TQ = 256
DSA_ROWS = 64
INT_MIN = -(2 ** 31)


def _count_lanes(hit):
    acc = hit[:, :LANES]
    for j in range(1, hit.shape[1] // LANES):
        acc = acc + hit[:, j * LANES:(j + 1) * LANES]
    return jnp.sum(acc, axis=-1, keepdims=True)


def _dsa_score_pair(i, qi_ref, ki_ref, wi_ref, score_ref):
    n_keys = (i + 1) * DSA_TQ
    logit = _dot_nt(_stack_heads(qi_ref[...]), ki_ref[0:n_keys, :])
    wi = wi_ref[...]
    part = wi[:, 0:1] * jnp.maximum(logit[:DSA_TQ], 0.0) + wi[:, 1:2] * jnp.maximum(logit[DSA_TQ:], 0.0)
    pair = pl.program_id(2)

    @pl.when(pair == 0)
    def _():
        score_ref[:, 0:n_keys] = part

    @pl.when(pair > 0)
    def _():
        score_ref[:, 0:n_keys] += part


def _dsa_thresholds(jobs):
    def as_float(code):
        return pltpu.bitcast(code ^ ((code >> 31) & 0x7FFFFFFF), F32)

    n_rg = DSA_TQ // DSA_ROWS
    units = [(i, score_ref, rg) for i, score_ref in jobs for rg in range(n_rg)]

    def step(unit, t_old, cand):
        i, score_ref, rg = unit
        k = score_ref[rg * DSA_ROWS:(rg + 1) * DSA_ROWS, 0:(i + 1) * DSA_TQ]
        n = _count_lanes(jnp.where(k >= as_float(cand), 1.0, 0.0))
        return jnp.where(n >= DSA_TOPK, cand, t_old)

    def bit_body(ib, ts):
        bit = jnp.left_shift(jnp.int32(1), 30 - ib)
        return tuple(step(u, t, t | bit) for u, t in zip(units, ts))

    ts = tuple(step(u, jnp.full((DSA_ROWS, 1), INT_MIN, I32), jnp.zeros((DSA_ROWS, 1), I32)) for u in units)
    ts = lax.fori_loop(0, 31, bit_body, ts)
    out = []
    for j in range(len(jobs)):
        code = jnp.concatenate(ts[j * n_rg:(j + 1) * n_rg], axis=0)
        out.append(jnp.where(code == INT_MIN, -jnp.inf, as_float(code)))
    return out


def _dsa_write_bias(i, score_ref, thr, bias_ref, first_row):
    n_keys = (i + 1) * DSA_TQ
    rows = slice(first_row, first_row + DSA_TQ)
    row = lax.broadcasted_iota(I32, (DSA_TQ, DSA_TQ), 0)
    col = lax.broadcasted_iota(I32, (DSA_TQ, DSA_TQ), 1)
    keys = score_ref[:, 0:n_keys]
    need = float(DSA_TOPK) - _count_lanes(jnp.where(keys > thr, 1.0, 0.0))
    strictly_before = jnp.where(row < col, 1.0, 0.0).astype(BF16)
    ties_seen = jnp.zeros((DSA_TQ, 1), F32)
    for t in range(i + 1):
        k = keys[:, t * DSA_TQ:(t + 1) * DSA_TQ]
        tie = jnp.where(k == thr, 1.0, 0.0)
        ties_before = ties_seen + _dot(tie.astype(BF16), strictly_before)
        chosen = (k > thr) | ((k == thr) & (ties_before < need))
        if t == i:
            chosen = chosen & (col <= row)
        bias_ref[rows, t * DSA_TQ:(t + 1) * DSA_TQ] = jnp.where(chosen, 0.0, NEG).astype(bias_ref.dtype)
        ties_seen = ties_seen + jnp.sum(tie, axis=-1, keepdims=True)
    if n_keys < bias_ref.shape[1]:
        bias_ref[rows, n_keys:] = jnp.full((DSA_TQ, bias_ref.shape[1] - n_keys), NEG, bias_ref.dtype)


def _dsa_select_kernel(qi_lo_ref, qi_hi_ref, ki_ref, wi_lo_ref, wi_hi_ref, bias_ref, score_lo_ref, score_hi_ref):
    n_chunks = ki_ref.shape[0] // DSA_TQ
    for j in range(n_chunks // 2):
        @pl.when(pl.program_id(1) == j)
        def _(j=j):
            lo, hi = j, n_chunks - 1 - j
            _dsa_score_pair(lo, qi_lo_ref, ki_ref, wi_lo_ref, score_lo_ref)
            _dsa_score_pair(hi, qi_hi_ref, ki_ref, wi_hi_ref, score_hi_ref)

            @pl.when(pl.program_id(2) == N_PAIRS - 1)
            def _():
                row = lax.broadcasted_iota(I32, (DSA_TQ, DSA_TQ), 0)
                col = lax.broadcasted_iota(I32, (DSA_TQ, DSA_TQ), 1)
                jobs = [(lo, score_lo_ref), (hi, score_hi_ref)]
                for i, score_ref in jobs:
                    diag = slice(i * DSA_TQ, (i + 1) * DSA_TQ)
                    score_ref[:, diag] = jnp.where(col <= row, score_ref[:, diag], NEG)
                for k, ((i, score_ref), thr) in enumerate(zip(jobs, _dsa_thresholds(jobs))):
                    _dsa_write_bias(i, score_ref, thr, bias_ref, k * DSA_TQ)


def _dsa_bias_block(c, n_chunks):
    return jnp.where(c < n_chunks // 2, 2 * c, 2 * (n_chunks - 1 - c) + 1)


def _dsa_select(qi, ki2, wi, batch, seq):
    assert seq % (2 * DSA_TQ) == 0 and min(DSA_TOPK, seq // 4) == DSA_TOPK
    n_chunks = seq // DSA_TQ
    lo_map = lambda b, j, p: (b * n_chunks + j, p)
    hi_map = lambda b, j, p: (b * n_chunks + n_chunks - 1 - j, p)
    return pl.pallas_call(
        _dsa_select_kernel,
        grid=(batch, n_chunks // 2, N_PAIRS),
        in_specs=[pl.BlockSpec((DSA_TQ, LANES), lo_map), pl.BlockSpec((DSA_TQ, LANES), hi_map),
                  pl.BlockSpec((seq, LANES), lambda b, j, p: (b, 0)),
                  pl.BlockSpec((DSA_TQ, LANES), lo_map), pl.BlockSpec((DSA_TQ, LANES), hi_map)],
        out_specs=pl.BlockSpec((2 * DSA_TQ, seq), lambda b, j, p: (b * (n_chunks // 2) + j, 0)),
        out_shape=jax.ShapeDtypeStruct((batch * seq, seq), BF16),
        scratch_shapes=[pltpu.VMEM((DSA_TQ, seq), F32), pltpu.VMEM((DSA_TQ, seq), F32)],
        compiler_params=_cparams("parallel", "arbitrary", "arbitrary"),
        name="dsa_select",
    )(qi, qi, ki2, wi, wi)


def _dsa_attend_kernel(q_ref, k_ref, v_ref, bias_ref, o_ref):
    for i in range(k_ref.shape[0] // DSA_TQ):
        @pl.when(pl.program_id(1) == i)
        def _(i=i):
            n_keys = (i + 1) * DSA_TQ
            bias = bias_ref[:, 0:n_keys].astype(F32)
            bias = jnp.concatenate([bias, bias], axis=0)
            for u in range(PAIRS_PER_STEP):
                pair = slice(u * LANES, (u + 1) * LANES)
                s = _dot_nt(_stack_heads(q_ref[:, pair]), k_ref[0:n_keys, pair]) + bias
                o_ref[:, pair] = _unstack_heads(_masked_attention(s, v_ref[0:n_keys, pair])).astype(o_ref.dtype)


def _dsa_attend(qb, kb, vb, bias, batch, seq):
    n_chunks = seq // DSA_TQ
    q_map = lambda b, c, p: (b * n_chunks + c, p)
    kv_map = lambda b, c, p: (b, p)
    return pl.pallas_call(
        _dsa_attend_kernel,
        grid=(batch, n_chunks, N_PAIRS // PAIRS_PER_STEP),
        in_specs=[pl.BlockSpec((DSA_TQ, STEP_LANES), q_map),
                  pl.BlockSpec((seq, STEP_LANES), kv_map),
                  pl.BlockSpec((seq, STEP_LANES), kv_map),
                  pl.BlockSpec((DSA_TQ, seq), lambda b, c, p: (b * n_chunks + _dsa_bias_block(c, n_chunks), 0))],
        out_specs=pl.BlockSpec((DSA_TQ, STEP_LANES), q_map),
        out_shape=jax.ShapeDtypeStruct((batch * seq, WIDTH), BF16),
        compiler_params=_cparams("parallel", "parallel", "arbitrary"),
        name="dsa_attend",
    )(qb, kb, vb, bias)


MIX_TM = 512
ROUTE_ROWS = 8


def _mix_out_kernel(x_ref, oa_ref, ob_ref, sga_ref, sgb_ref, wa_ref, wb_ref, wo_ref, g_ref, wr_ref, br_ref,
                    x1_ref, xs_ref, route_ref, route_t_ref):
    mixed = (sga_ref[...].astype(F32) * _dot(oa_ref[...], wa_ref[...])
             + sgb_ref[...].astype(F32) * _dot(ob_ref[...], wb_ref[...]))
    x1 = x_ref[...] + _dot(mixed.astype(BF16), wo_ref[...])
    x1_ref[...] = x1
    hn = x1 * lax.rsqrt(jnp.mean(x1 * x1, axis=-1, keepdims=True) + RMS_EPS) * g_ref[...]
    xs_ref[...] = hn

    hi = hn.astype(BF16)
    lo = (hn - hi.astype(F32)).astype(BF16)
    both = _dot(hi, wr_ref[...])
    logits = both[:, :LANES] + both[:, LANES:] + _dot(lo, wr_ref[:, :LANES]) + br_ref[...]
    lane = lax.broadcasted_iota(I32, logits.shape, 1)
    far = jnp.int32(LANES)

    def first_lane_of_max(v, valid):
        top = jnp.max(jnp.where(valid, v, NEG), axis=-1, keepdims=True)
        return top, jnp.min(jnp.where(valid & (v == top), lane, far), axis=-1, keepdims=True)

    is_group = lane < N_GROUPS
    g_max, g_sel = first_lane_of_max(logits, is_group)
    g_w = 1.0 / jnp.sum(jnp.where(is_group, jnp.exp(logits - g_max), 0.0), axis=-1, keepdims=True)
    first = N_GROUPS + g_sel * EXPERTS_PER_GROUP
    in_group = (lane >= first) & (lane < first + EXPERTS_PER_GROUP)
    e_max, _ = first_lane_of_max(logits, in_group)
    e_exp = jnp.where(in_group, jnp.exp(logits - e_max), 0.0)
    prob = e_exp / jnp.sum(e_exp, axis=-1, keepdims=True)
    p1, i1 = first_lane_of_max(prob, in_group)
    p2, i2 = first_lane_of_max(prob, in_group & (lane != i1))
    denom = p1 + p2
    record = jnp.where(lane == 0, (i1 - N_GROUPS).astype(F32),
                       jnp.where(lane == 1, (i2 - N_GROUPS).astype(F32),
                                 jnp.where(lane == 2, g_w * p1 / denom,
                                           jnp.where(lane == 3, g_w * p2 / denom, 0.0))))
    route_ref[...] = record
    route_t_ref[...] = record.T[:ROUTE_ROWS, :]


def _mix_out(x2, oa, ob, sga, sgb, wa, wb, wo, g_ffn, wr, br):
    t = x2.shape[0]
    row = lambda i: (i, 0)
    const = lambda i: (0, 0)
    once = dict(pipeline_mode=pl.Buffered(1))
    return pl.pallas_call(
        _mix_out_kernel,
        grid=(t // MIX_TM,),
        in_specs=[pl.BlockSpec((MIX_TM, D_MODEL), row),
                  pl.BlockSpec((MIX_TM, WIDTH), row), pl.BlockSpec((MIX_TM, WIDTH), row),
                  pl.BlockSpec((MIX_TM, D_MODEL), row), pl.BlockSpec((MIX_TM, D_MODEL), row),
                  pl.BlockSpec((WIDTH, D_MODEL), const, **once), pl.BlockSpec((WIDTH, D_MODEL), const, **once),
                  pl.BlockSpec((D_MODEL, D_MODEL), const, **once), pl.BlockSpec((1, D_MODEL), const),
                  pl.BlockSpec((D_MODEL, 2 * LANES), const, **once),
                  pl.BlockSpec((1, LANES), const)],
        out_specs=[pl.BlockSpec((MIX_TM, D_MODEL), row), pl.BlockSpec((MIX_TM, D_MODEL), row),
                   pl.BlockSpec((MIX_TM, LANES), row), pl.BlockSpec((ROUTE_ROWS, MIX_TM), lambda i: (0, i))],
        out_shape=[jax.ShapeDtypeStruct((t, D_MODEL), F32), jax.ShapeDtypeStruct((t, D_MODEL), F32),
                   jax.ShapeDtypeStruct((t, LANES), F32), jax.ShapeDtypeStruct((ROUTE_ROWS, t), F32)],
        compiler_params=_cparams("parallel"),
        name="mix_out",
    )(x2, oa, ob, sga, sgb, wa, wb, wo, g_ffn, wr, br)


PLAN_TM = 512


def _moe_plan_kernel(route_t_ref, dest_ref, counts_ref, count_ref, start_ref):
    phase = pl.program_id(0)
    step = pl.program_id(1)
    expert = lax.broadcasted_iota(I32, (LANES, PLAN_TM), 0)
    e1 = route_t_ref[0:1, :].astype(I32)
    e2 = route_t_ref[1:2, :].astype(I32)
    hot1 = expert == e1
    hot2 = expert == e2
    hot = jnp.where(hot1 | hot2, 1.0, 0.0)

    @pl.when((phase == 0) & (step == 0))
    def _():
        count_ref[...] = jnp.zeros_like(count_ref)

    @pl.when(phase == 0)
    def _():
        count_ref[...] += jnp.sum(hot, axis=-1, keepdims=True)
        dest_ref[...] = jnp.zeros_like(dest_ref)

    @pl.when((phase == 1) & (step == 0))
    def _():
        counts = jnp.broadcast_to(count_ref[...], (LANES, LANES))
        counts_ref[...] = counts
        padded = jnp.ceil(counts / DISPATCH_BLOCK) * DISPATCH_BLOCK
        sub = lax.broadcasted_iota(I32, (LANES, LANES), 0)
        ends = padded
        shift = 1
        while shift < LANES:
            ends = ends + jnp.where(sub >= shift, pltpu.roll(ends, shift, 0), 0.0)
            shift *= 2
        start_ref[...] = (ends - padded)[:, 0:1]
        count_ref[...] = jnp.zeros_like(count_ref)

    @pl.when(phase == 1)
    def _():
        tok_r = lax.broadcasted_iota(I32, (PLAN_TM, PLAN_TM), 0)
        tok_c = lax.broadcasted_iota(I32, (PLAN_TM, PLAN_TM), 1)
        earlier = jnp.where(tok_r < tok_c, 1.0, 0.0).astype(BF16)
        slot = start_ref[...] + count_ref[...] + _dot(hot.astype(BF16), earlier)
        d1 = jnp.sum(jnp.where(hot1, slot, 0.0), axis=0, keepdims=True)
        d2 = jnp.sum(jnp.where(hot2, slot, 0.0), axis=0, keepdims=True)
        sub = lax.broadcasted_iota(I32, (ROUTE_ROWS, PLAN_TM), 0)
        dest_ref[...] = jnp.where(sub == 0, d1, jnp.where(sub == 1, d2, 0.0)).astype(I32)
        count_ref[...] += jnp.sum(hot, axis=-1, keepdims=True)


def _moe_plan(route_t):
    t = route_t.shape[1]
    return pl.pallas_call(
        _moe_plan_kernel,
        grid=(2, t // PLAN_TM),
        in_specs=[pl.BlockSpec((ROUTE_ROWS, PLAN_TM), lambda ph, i: (0, i))],
        out_specs=[pl.BlockSpec((ROUTE_ROWS, PLAN_TM), lambda ph, i: (0, i * ph)),
                   pl.BlockSpec((LANES, LANES), lambda ph, i: (0, 0))],
        out_shape=[jax.ShapeDtypeStruct((ROUTE_ROWS, t), I32), jax.ShapeDtypeStruct((LANES, LANES), F32)],
        scratch_shapes=[pltpu.VMEM((LANES, 1), F32), pltpu.VMEM((LANES, 1), F32)],
        compiler_params=_cparams("arbitrary", "arbitrary"),
        name="moe_plan",
    )(route_t)


DISP_TM = 1024
DMA_UNROLL = 8


def _moe_dispatch_kernel(d1_ref, d2_ref, xs_ref, zeros_ref, out_ref, sem):
    del zeros_ref

    def row_copy(r, dest):
        return pltpu.make_async_copy(xs_ref.at[pl.ds(r, 1), :], out_ref.at[pl.ds(dest, 1), :], sem)

    def issue(r, carry):
        row_copy(r, d1_ref[r]).start(priority=0)
        row_copy(r, d2_ref[r]).start(priority=1)
        return carry
    lax.fori_loop(0, DISP_TM, issue, 0, unroll=DMA_UNROLL)

    for _ in range(2):
        pltpu.make_async_copy(xs_ref, out_ref.at[pl.ds(0, DISP_TM), :], sem).wait()


def _moe_dispatch(d1, d2, xs, n_rows):
    t = xs.shape[0]
    smem = lambda: pl.BlockSpec((DISP_TM,), lambda i: (i,), memory_space=pltpu.SMEM)
    return pl.pallas_call(
        _moe_dispatch_kernel,
        grid=(t // DISP_TM,),
        in_specs=[smem(), smem(), pl.BlockSpec((DISP_TM, D_MODEL), lambda i: (i, 0)),
                  pl.BlockSpec(memory_space=pl.ANY)],
        out_specs=pl.BlockSpec(memory_space=pl.ANY),
        out_shape=jax.ShapeDtypeStruct((n_rows, D_MODEL), F32),
        scratch_shapes=[pltpu.SemaphoreType.DMA(())],
        input_output_aliases={3: 0},
        compiler_params=_cparams("arbitrary"),
        name="moe_dispatch",
    )(d1, d2, xs, jnp.zeros((n_rows, D_MODEL), F32))


def _moe_expert_kernel(block_expert_ref, n_used_ref, xs_ref, w1_ref, w3_ref, w2_ref, ys_ref):
    del block_expert_ref
    live = pl.program_id(0) < n_used_ref[0]

    @pl.when(live)
    def _():
        x = xs_ref[...].astype(BF16)
        h1 = _dot(x, w1_ref[0])
        h3 = _dot(x, w3_ref[0])
        hid = h1 / (1.0 + jnp.exp(-h1)) * h3
        ys_ref[...] = _dot(hid.astype(BF16), w2_ref[0])

    @pl.when(jnp.logical_not(live))
    def _():
        ys_ref[...] = jnp.zeros_like(ys_ref)


def _moe_experts(block_expert, n_used, xs_sorted, w1, w3, w2):
    n_rows = xs_sorted.shape[0]
    grid_spec = pltpu.PrefetchScalarGridSpec(
        num_scalar_prefetch=2,
        grid=(n_rows // DISPATCH_BLOCK,),
        in_specs=[pl.BlockSpec((DISPATCH_BLOCK, D_MODEL), lambda j, be, nu: (j, 0)),
                  pl.BlockSpec((1, D_MODEL, EXPERT_FF), lambda j, be, nu: (be[j], 0, 0)),
                  pl.BlockSpec((1, D_MODEL, EXPERT_FF), lambda j, be, nu: (be[j], 0, 0)),
                  pl.BlockSpec((1, EXPERT_FF, D_MODEL), lambda j, be, nu: (be[j], 0, 0))],
        out_specs=pl.BlockSpec((DISPATCH_BLOCK, D_MODEL), lambda j, be, nu: (j, 0)))
    return pl.pallas_call(
        _moe_expert_kernel,
        grid_spec=grid_spec,
        out_shape=jax.ShapeDtypeStruct((n_rows, D_MODEL), F32),
        compiler_params=_cparams("arbitrary"),
        name="moe_experts",
    )(block_expert, n_used, xs_sorted, w1, w3, w2)


COMB_TM = 512


def _moe_combine_kernel(d1_ref, d2_ref, x1_ref, route_ref, g_ref, ys_ref, out_ref, y1_ref, y2_ref, sem):
    def row_copy(src, r, buf):
        return pltpu.make_async_copy(ys_ref.at[pl.ds(src, 1), :], buf.at[pl.ds(r, 1), :], sem)

    def issue(r, carry):
        row_copy(d1_ref[r], r, y1_ref).start(priority=0)
        row_copy(d2_ref[r], r, y2_ref).start(priority=1)
        return carry
    lax.fori_loop(0, COMB_TM, issue, 0, unroll=DMA_UNROLL)

    for buf in (y1_ref, y2_ref):
        pltpu.make_async_copy(ys_ref.at[pl.ds(0, COMB_TM), :], buf, sem).wait()

    route = route_ref[...]
    x2 = x1_ref[...] + (route[:, 2:3] * y1_ref[...] + route[:, 3:4] * y2_ref[...])
    out_ref[...] = x2 * lax.rsqrt(jnp.mean(x2 * x2, axis=-1, keepdims=True) + RMS_EPS) * g_ref[...]


def _moe_combine(d1, d2, x1, route, g_final, ys):
    t = x1.shape[0]
    smem = lambda: pl.BlockSpec((COMB_TM,), lambda i: (i,), memory_space=pltpu.SMEM)
    row = lambda i: (i, 0)
    return pl.pallas_call(
        _moe_combine_kernel,
        grid=(t // COMB_TM,),
        in_specs=[smem(), smem(), pl.BlockSpec((COMB_TM, D_MODEL), row), pl.BlockSpec((COMB_TM, LANES), row),
                  pl.BlockSpec((1, D_MODEL), lambda i: (0, 0)), pl.BlockSpec(memory_space=pl.ANY)],
        out_specs=pl.BlockSpec((COMB_TM, D_MODEL), row),
        out_shape=jax.ShapeDtypeStruct((t, D_MODEL), F32),
        scratch_shapes=[pltpu.VMEM((COMB_TM, D_MODEL), F32), pltpu.VMEM((COMB_TM, D_MODEL), F32),
                        pltpu.SemaphoreType.DMA(())],
        compiler_params=_cparams("arbitrary"),
        name="moe_combine",
    )(d1, d2, x1, route, g_final, ys)


def _hier_moe_tail(x1, xs, route, route_t, w1, w3, w2, g_final):
    t = x1.shape[0]
    dest, counts = _moe_plan(route_t)
    counts = counts[:N_EXPERTS, 0].astype(I32)
    padded = (counts + DISPATCH_BLOCK - 1) // DISPATCH_BLOCK * DISPATCH_BLOCK
    ends = jnp.cumsum(padded)
    n_blocks = (t * 2) // DISPATCH_BLOCK + N_EXPERTS
    block_start = jnp.arange(n_blocks, dtype=I32) * DISPATCH_BLOCK
    block_expert = jnp.minimum(jnp.sum((ends[None, :] <= block_start[:, None]).astype(I32), axis=1), N_EXPERTS - 1)
    n_used = (ends[-1:] // DISPATCH_BLOCK).astype(I32)
    d1, d2 = dest[0], dest[1]
    xs_sorted = _moe_dispatch(d1, d2, xs, n_blocks * DISPATCH_BLOCK)
    ys = _moe_experts(block_expert, n_used, xs_sorted, w1, w3, w2)
    return _moe_combine(d1, d2, x1, route, g_final, ys)


def kernel(x, g_mix, w_in, w_proj_a, w_proj_b, w_out, g_ffn, w_group, b_group, w_expert, b_expert, w1, w3, w2, g_final):
    batch, seq, d = x.shape
    assert d == D_MODEL and g_mix.shape[0] == 1, "one layer of width D_MODEL"
    x2 = x.reshape(batch * seq, d)
    cos_t, sin_t = _rope_tables(seq)
    qa, ka, va, qb, kb, vb, qi, ki2, wi, sga, sgb, kmean = _in_proj(
        x2, g_mix[0][None, :], _arrange_w_in(w_in[0]), cos_t, sin_t, seq)
    selb = _moba_select(qa, kmean, batch, seq)
    oa = _moba_attn(qa, ka, va, selb, batch, seq)
    ob = _dsa_attend(qb, kb, vb, _dsa_select(qi, ki2, wi, batch, seq), batch, seq)
    wr, br = _router_params(w_group[0], b_group[0], w_expert[0], b_expert[0])
    x1, xs, route, route_t = _mix_out(
        x2, oa, ob, sga, sgb, w_proj_a[0].astype(BF16), w_proj_b[0].astype(BF16), w_out[0].astype(BF16),
        g_ffn[0][None, :], wr, br)
    out = _hier_moe_tail(x1, xs, route, route_t, w1[0].astype(BF16), w3[0].astype(BF16), w2[0].astype(BF16),
                         g_final[None, :])
    return out.reshape(batch, seq, d)


def _router_params(w_group, b_group, w_expert, b_expert):
    w = jnp.pad(jnp.concatenate([w_group, w_expert], axis=1), ((0, 0), (0, LANES - N_GROUPS - N_EXPERTS)))
    b = jnp.pad(jnp.concatenate([b_group, b_expert]), (0, LANES - N_GROUPS - N_EXPERTS))[None, :]
    w_hi = w.astype(BF16)
    w_lo = (w - w_hi.astype(F32)).astype(BF16)
    return jnp.concatenate([w_hi, w_lo], axis=1), b
```

```python
import functools

import jax
import jax.numpy as jnp
from jax import lax
from jax.experimental import pallas as pl
from jax.experimental.pallas import tpu as pltpu

F32 = jnp.float32
BF16 = jnp.bfloat16
I32 = jnp.int32

D_MODEL = 1024
HEAD_DIM = 64
N_HEADS = 8
WIDTH = N_HEADS * HEAD_DIM
N_PAIRS = N_HEADS // 2
MOBA_BLOCK = 256
MOBA_TOPK = 3
DSA_TOPK = 256
IDX_SCALE = float(WIDTH) ** -0.5
ATTN_SCALE = float(HEAD_DIM) ** -0.5
N_GROUPS = 4
EXPERTS_PER_GROUP = 8
N_EXPERTS = N_GROUPS * EXPERTS_PER_GROUP
EXPERT_FF = 512
DISPATCH_BLOCK = 512
ROPE_THETA = 10000.0
RMS_EPS = 1e-6
NEG = -1e30

LANES = 128
VMEM_LIMIT = 56 * 1024 * 1024

C_QA, C_KA, C_VA, C_QB, C_KB, C_VB, C_QI = (i * WIDTH for i in range(7))
C_KI = 7 * WIDTH
C_WI = C_KI + LANES
C_GA = C_WI + WIDTH
C_GB = C_GA + D_MODEL
IN_COLS_PADDED = C_GB + D_MODEL


def _cparams(*semantics):
    return pltpu.CompilerParams(dimension_semantics=semantics, vmem_limit_bytes=VMEM_LIMIT)


def _dot_nt(a, b):
    return lax.dot_general(a, b, (((1,), (1,)), ((), ())), preferred_element_type=F32)


def _dot(a, b):
    return jnp.dot(a, b, preferred_element_type=F32)


IN_TM = 512


def _in_proj_kernel(x_ref, g_ref, w_ref, cos_ref, sin_ref,
                    qa_ref, ka_ref, va_ref, qb_ref, kb_ref, vb_ref, qi_ref, ki_ref, wi_ref,
                    sga_ref, sgb_ref, kmean_ref):
    x = x_ref[...]
    h = x * lax.rsqrt(jnp.mean(x * x, axis=-1, keepdims=True) + RMS_EPS) * g_ref[...]
    hb = h.astype(BF16)
    cos = cos_ref[...]
    sin = sin_ref[...]
    upper_half = (lax.broadcasted_iota(I32, (IN_TM, LANES), 1) & (HEAD_DIM // 2)) != 0

    def rope(v):
        partner = jnp.where(upper_half, pltpu.roll(v, HEAD_DIM // 2, 1), pltpu.roll(v, LANES - HEAD_DIM // 2, 1))
        return v * cos + partner * sin

    def proj(c0, width):
        return _dot(hb, w_ref[:, c0:c0 + width])

    def store_heads(ref, c0, rotary):
        r = proj(c0, WIDTH)
        for j in range(WIDTH // LANES):
            v = r[:, j * LANES:(j + 1) * LANES]
            ref[:, j * LANES:(j + 1) * LANES] = (rope(v) if rotary else v).astype(ref.dtype)

    store_heads(qa_ref, C_QA, True)
    store_heads(va_ref, C_VA, False)
    store_heads(qb_ref, C_QB, True)
    store_heads(kb_ref, C_KB, True)
    store_heads(vb_ref, C_VB, False)
    store_heads(qi_ref, C_QI, True)

    r = proj(C_KA, WIDTH)
    for j in range(WIDTH // LANES):
        v = rope(r[:, j * LANES:(j + 1) * LANES])
        ka_ref[:, j * LANES:(j + 1) * LANES] = v.astype(BF16)
        for blk in range(IN_TM // MOBA_BLOCK):
            kmean_ref[blk, :, j * LANES:(j + 1) * LANES] = jnp.mean(
                v[blk * MOBA_BLOCK:(blk + 1) * MOBA_BLOCK], axis=0, keepdims=True)

    ki_ref[...] = rope(proj(C_KI, LANES)).astype(BF16)
    wi_ref[...] = proj(C_WI, WIDTH) * IDX_SCALE
    sga_ref[...] = (1.0 / (1.0 + jnp.exp(-proj(C_GA, D_MODEL)))).astype(BF16)
    sgb_ref[...] = (1.0 / (1.0 + jnp.exp(-proj(C_GB, D_MODEL)))).astype(BF16)


def _in_proj(x2, g_mix, w_r, cos_t, sin_t, seq):
    t = x2.shape[0]
    n_tiles = t // IN_TM
    tiles_per_seq = seq // IN_TM
    row = lambda i: (i, 0)
    const = lambda i: (0, 0)
    act = lambda w, dt: jax.ShapeDtypeStruct((t, w), dt)
    out_shape = ([act(WIDTH, BF16)] * 7 + [act(LANES, BF16), act(WIDTH, F32), act(D_MODEL, BF16), act(D_MODEL, BF16),
                                           jax.ShapeDtypeStruct((t // MOBA_BLOCK, 1, WIDTH), F32)])
    out_specs = ([pl.BlockSpec((IN_TM, WIDTH), row)] * 7
                 + [pl.BlockSpec((IN_TM, LANES), row), pl.BlockSpec((IN_TM, WIDTH), row),
                    pl.BlockSpec((IN_TM, D_MODEL), row), pl.BlockSpec((IN_TM, D_MODEL), row),
                    pl.BlockSpec((IN_TM // MOBA_BLOCK, 1, WIDTH), lambda i: (i, 0, 0))])
    return pl.pallas_call(
        _in_proj_kernel,
        grid=(n_tiles,),
        in_specs=[pl.BlockSpec((IN_TM, D_MODEL), row),
                  pl.BlockSpec((1, D_MODEL), const),
                  pl.BlockSpec((D_MODEL, IN_COLS_PADDED), const, pipeline_mode=pl.Buffered(1)),
                  pl.BlockSpec((IN_TM, LANES), lambda i: (i % tiles_per_seq, 0)),
                  pl.BlockSpec((IN_TM, LANES), lambda i: (i % tiles_per_seq, 0))],
        out_specs=out_specs,
        out_shape=out_shape,
        compiler_params=_cparams("parallel"),
        name="in_proj",
    )(x2, g_mix, w_r, cos_t, sin_t)


def _moba_select_kernel(qa_ref, km_ref, selb_ref):
    own = pl.program_id(1)
    n_blocks = km_ref.shape[1]
    km = km_ref[0]
    col_head = lax.broadcasted_iota(I32, (N_HEADS, WIDTH), 1) // HEAD_DIM
    head_mask = col_head == lax.broadcasted_iota(I32, (N_HEADS, WIDTH), 0)
    rows = [jnp.where(head_mask, km[n:n + 1, :], 0.0) for n in range(n_blocks)]
    rows.append(jnp.zeros((LANES - n_blocks * N_HEADS, WIDTH), F32))
    km_t = jnp.concatenate(rows, axis=0).astype(BF16)
    gate = _dot_nt(qa_ref[...], km_t)
    lane = lax.broadcasted_iota(I32, gate.shape, 1)
    past = (lane // N_HEADS) < own
    g = jnp.where(past, gate, NEG)
    rank = jnp.zeros(gate.shape, I32)
    for r in range(1, n_blocks):
        later = pltpu.roll(g, LANES - N_HEADS * r, 1)
        earlier = pltpu.roll(g, N_HEADS * r, 1)
        rank = rank + (later > g).astype(I32) + (earlier >= g).astype(I32)
    chosen = jnp.where(past & (rank < MOBA_TOPK), 1.0, 0.0).astype(BF16)
    src = lax.broadcasted_iota(I32, (LANES, N_PAIRS * LANES), 0)
    dst = lax.broadcasted_iota(I32, (LANES, N_PAIRS * LANES), 1)
    src_n, src_h = src // N_HEADS, src % N_HEADS
    dst_pair, dst_w = dst // LANES, dst % LANES
    expand = ((src_n < n_blocks) & (dst_w < 2 * N_HEADS) & (dst_w % N_HEADS == src_n)
              & (dst_pair * 2 + dst_w // N_HEADS == src_h))
    hit = _dot(chosen, jnp.where(expand, 1.0, 0.0).astype(BF16))
    selb_ref[...] = jnp.where(hit > 0.5, 0.0, NEG).astype(selb_ref.dtype)


def _moba_select(qa, kmean, batch, seq):
    n_blocks = seq // MOBA_BLOCK
    assert n_blocks * N_HEADS <= LANES and n_blocks <= N_HEADS
    t = batch * seq
    return pl.pallas_call(
        _moba_select_kernel,
        grid=(batch, n_blocks),
        in_specs=[pl.BlockSpec((MOBA_BLOCK, WIDTH), lambda b, i: (b * n_blocks + i, 0)),
                  pl.BlockSpec((1, n_blocks, WIDTH), lambda b, i: (b, 0, 0))],
        out_specs=pl.BlockSpec((MOBA_BLOCK, N_PAIRS * LANES), lambda b, i: (b * n_blocks + i, 0)),
        out_shape=jax.ShapeDtypeStruct((t, N_PAIRS * LANES), BF16),
        compiler_params=_cparams("parallel", "parallel"),
        name="moba_select",
    )(qa, kmean.reshape(batch, n_blocks, WIDTH))


def _stack_heads(q2):
    lane = lax.broadcasted_iota(I32, q2.shape, 1)
    zero = jnp.zeros_like(q2)
    return jnp.concatenate([jnp.where(lane < HEAD_DIM, q2, zero), jnp.where(lane >= HEAD_DIM, q2, zero)], axis=0)


def _unstack_heads(o):
    rows = o.shape[0] // 2
    lane = lax.broadcasted_iota(I32, (rows, LANES), 1)
    return jnp.where(lane < HEAD_DIM, o[:rows], o[rows:])


def _attend_pairs(pairs, scores, v_ref, n_keys, o_ref):
    tops = [jnp.max(s, axis=-1, keepdims=True) for s in scores]
    probs = [jnp.exp(s - m) for s, m in zip(scores, tops)]
    outs = [_dot(p.astype(BF16), v_ref[0:n_keys, pair]) for pair, p in zip(pairs, probs)]
    sums = [jnp.sum(p, axis=-1, keepdims=True) for p in probs]
    for pair, o, l in zip(pairs, outs, sums):
        o_ref[:, pair] = _unstack_heads(o / l).astype(o_ref.dtype)


PAIRS_PER_STEP = 2
STEP_LANES = PAIRS_PER_STEP * LANES


def _moba_attn_kernel(q_ref, k_ref, v_ref, selb_ref, o_ref):
    own = pl.program_id(2)
    n_blocks = k_ref.shape[0] // MOBA_BLOCK
    q_pos = lax.broadcasted_iota(I32, (2 * MOBA_BLOCK, MOBA_BLOCK), 0) % MOBA_BLOCK
    causal = lax.broadcasted_iota(I32, (2 * MOBA_BLOCK, MOBA_BLOCK), 1) <= q_pos
    for i in range(n_blocks):
        @pl.when(own == i)
        def _(i=i):
            n_keys = (i + 1) * MOBA_BLOCK
            key_blk = lax.broadcasted_iota(I32, (n_keys, LANES), 0) // MOBA_BLOCK
            key_lane = lax.broadcasted_iota(I32, (n_keys, LANES), 1)
            indicator = jnp.where((key_lane < 2 * N_HEADS) & (key_lane % N_HEADS == key_blk) & (key_blk < i),
                                  1.0, 0.0).astype(BF16)
            q_lane = lax.broadcasted_iota(I32, (MOBA_BLOCK, LANES), 1)
            pairs = [slice(u * LANES, (u + 1) * LANES) for u in range(PAIRS_PER_STEP)]
            scores = []
            for pair in pairs:
                selb = selb_ref[:, pair]
                zero = jnp.zeros_like(selb)
                bias_rows = jnp.concatenate([jnp.where(q_lane < N_HEADS, selb, zero),
                                             jnp.where((q_lane >= N_HEADS) & (q_lane < 2 * N_HEADS), selb, zero)], axis=0)
                s = _dot_nt(jnp.concatenate([_stack_heads(q_ref[:, pair]), bias_rows], axis=1),
                            jnp.concatenate([k_ref[0:n_keys, pair], indicator], axis=1))
                own_s = jnp.where(causal, s[:, i * MOBA_BLOCK:], NEG)
                scores.append(own_s if i == 0 else jnp.concatenate([s[:, :i * MOBA_BLOCK], own_s], axis=1))
            _attend_pairs(pairs, scores, v_ref, n_keys, o_ref)


def _moba_attn(qa, ka, va, selb, batch, seq):
    n_blocks = seq // MOBA_BLOCK
    t = batch * seq
    q_map = lambda b, p, i: (b * n_blocks + i, p)
    kv_map = lambda b, p, i: (b, p)
    return pl.pallas_call(
        _moba_attn_kernel,
        grid=(batch, N_PAIRS // PAIRS_PER_STEP, n_blocks),
        in_specs=[pl.BlockSpec((MOBA_BLOCK, STEP_LANES), q_map),
                  pl.BlockSpec((seq, STEP_LANES), kv_map),
                  pl.BlockSpec((seq, STEP_LANES), kv_map),
                  pl.BlockSpec((MOBA_BLOCK, STEP_LANES), q_map)],
        out_specs=pl.BlockSpec((MOBA_BLOCK, STEP_LANES), q_map),
        out_shape=jax.ShapeDtypeStruct((t, WIDTH), BF16),
        compiler_params=_cparams("parallel", "parallel", "arbitrary"),
        name="moba_attn",
    )(qa, ka, va, selb)


def _rope_tables(seq):
    inv = jnp.power(ROPE_THETA, -jnp.arange(0, HEAD_DIM, 2, dtype=F32) / HEAD_DIM)
    ang = jnp.arange(seq, dtype=F32)[:, None] * inv[None, :]
    cos, sin = jnp.cos(ang), jnp.sin(ang)
    reps = LANES // HEAD_DIM
    return jnp.tile(jnp.concatenate([cos, cos], axis=-1), (1, reps)), jnp.tile(jnp.concatenate([-sin, sin], axis=-1), (1, reps))


def _arrange_w_in(w_in):
    sizes = (WIDTH,) * 7 + (HEAD_DIM, N_HEADS, D_MODEL, D_MODEL)
    offs = [0]
    for s in sizes:
        offs.append(offs[-1] + s)
    seg = [w_in[:, offs[i]:offs[i + 1]] for i in range(len(sizes))]
    qa, ka, va, qb, kb, vb, qi, ki, wi, ga, gb = seg
    wi_pairs = jnp.pad(wi.reshape(-1, N_PAIRS, 2), ((0, 0), (0, 0), (0, LANES - 2))).reshape(-1, WIDTH)
    w = jnp.concatenate([qa * ATTN_SCALE, ka, va, qb * ATTN_SCALE, kb, vb, qi, ki, ki, wi_pairs, ga, gb], axis=1)
    return w.astype(BF16)


DSA_TQ = 256
DSA_ROWS = 64
INT_MIN = -(2 ** 31)


def _count_lanes(hit):
    acc = hit[:, :LANES]
    for j in range(1, hit.shape[1] // LANES):
        acc = acc + hit[:, j * LANES:(j + 1) * LANES]
    return jnp.sum(acc, axis=-1, keepdims=True)


def _dsa_score_pair(jobs, ki_ref):
    logits = [_dot_nt(_stack_heads(qi_ref[...]), ki_ref[0:(i + 1) * DSA_TQ, :]) for i, qi_ref, _, _ in jobs]
    parts = []
    for (_, _, wi_ref, _), logit in zip(jobs, logits):
        wi = wi_ref[...]
        parts.append(wi[:, 0:1] * jnp.maximum(logit[:DSA_TQ], 0.0) + wi[:, 1:2] * jnp.maximum(logit[DSA_TQ:], 0.0))
    pair = pl.program_id(2)

    @pl.when(pair == 0)
    def _():
        for (i, _, _, score_ref), part in zip(jobs, parts):
            score_ref[:, 0:(i + 1) * DSA_TQ] = part

    @pl.when(pair > 0)
    def _():
        for (i, _, _, score_ref), part in zip(jobs, parts):
            score_ref[:, 0:(i + 1) * DSA_TQ] += part


def _dsa_thresholds(jobs):
    def as_float(code):
        return pltpu.bitcast(code ^ ((code >> 31) & 0x7FFFFFFF), F32)

    n_rg = DSA_TQ // DSA_ROWS
    units = [(i, score_ref, rg) for i, score_ref in jobs for rg in range(n_rg)]

    def step(unit, t_old, cand):
        i, score_ref, rg = unit
        k = score_ref[rg * DSA_ROWS:(rg + 1) * DSA_ROWS, 0:(i + 1) * DSA_TQ]
        n = _count_lanes(jnp.where(k >= as_float(cand), 1.0, 0.0))
        return jnp.where(n >= DSA_TOPK, cand, t_old)

    def bit_body(ib, ts):
        bit = jnp.left_shift(jnp.int32(1), 30 - ib)
        return tuple(step(u, t, t | bit) for u, t in zip(units, ts))

    ts = tuple(step(u, jnp.full((DSA_ROWS, 1), INT_MIN, I32), jnp.zeros((DSA_ROWS, 1), I32)) for u in units)
    ts = lax.fori_loop(0, 31, bit_body, ts)
    out = []
    for j in range(len(jobs)):
        code = jnp.concatenate(ts[j * n_rg:(j + 1) * n_rg], axis=0)
        out.append(jnp.where(code == INT_MIN, -jnp.inf, as_float(code)))
    return out


def _dsa_write_bias(i, score_ref, thr, bias_ref, first_row):
    n_keys = (i + 1) * DSA_TQ
    rows = slice(first_row, first_row + DSA_TQ)
    row = lax.broadcasted_iota(I32, (DSA_TQ, DSA_TQ), 0)
    col = lax.broadcasted_iota(I32, (DSA_TQ, DSA_TQ), 1)
    keys = score_ref[:, 0:n_keys]
    need = float(DSA_TOPK) - _count_lanes(jnp.where(keys > thr, 1.0, 0.0))
    strictly_before = jnp.where(row < col, 1.0, 0.0).astype(BF16)
    ties_seen = jnp.zeros((DSA_TQ, 1), F32)
    for t in range(i + 1):
        k = keys[:, t * DSA_TQ:(t + 1) * DSA_TQ]
        tie = jnp.where(k == thr, 1.0, 0.0)
        ties_before = ties_seen + _dot(tie.astype(BF16), strictly_before)
        chosen = (k > thr) | ((k == thr) & (ties_before < need))
        if t == i:
            chosen = chosen & (col <= row)
        bias_ref[rows, t * DSA_TQ:(t + 1) * DSA_TQ] = jnp.where(chosen, 0.0, NEG).astype(bias_ref.dtype)
        ties_seen = ties_seen + jnp.sum(tie, axis=-1, keepdims=True)
    if n_keys < bias_ref.shape[1]:
        bias_ref[rows, n_keys:] = jnp.full((DSA_TQ, bias_ref.shape[1] - n_keys), NEG, bias_ref.dtype)


def _dsa_select_kernel(qi_lo_ref, qi_hi_ref, ki_ref, wi_lo_ref, wi_hi_ref, bias_ref, score_lo_ref, score_hi_ref):
    n_chunks = ki_ref.shape[0] // DSA_TQ
    for j in range(n_chunks // 2):
        @pl.when(pl.program_id(1) == j)
        def _(j=j):
            lo, hi = j, n_chunks - 1 - j
            _dsa_score_pair([(lo, qi_lo_ref, wi_lo_ref, score_lo_ref), (hi, qi_hi_ref, wi_hi_ref, score_hi_ref)], ki_ref)

            @pl.when(pl.program_id(2) == N_PAIRS - 1)
            def _():
                row = lax.broadcasted_iota(I32, (DSA_TQ, DSA_TQ), 0)
                col = lax.broadcasted_iota(I32, (DSA_TQ, DSA_TQ), 1)
                jobs = [(lo, score_lo_ref), (hi, score_hi_ref)]
                for i, score_ref in jobs:
                    diag = slice(i * DSA_TQ, (i + 1) * DSA_TQ)
                    score_ref[:, diag] = jnp.where(col <= row, score_ref[:, diag], NEG)
                for k, ((i, score_ref), thr) in enumerate(zip(jobs, _dsa_thresholds(jobs))):
                    _dsa_write_bias(i, score_ref, thr, bias_ref, k * DSA_TQ)


def _dsa_bias_block(c, n_chunks):
    return jnp.where(c < n_chunks // 2, 2 * c, 2 * (n_chunks - 1 - c) + 1)


def _dsa_select(qi, ki2, wi, batch, seq):
    assert seq % (2 * DSA_TQ) == 0 and min(DSA_TOPK, seq // 4) == DSA_TOPK
    n_chunks = seq // DSA_TQ
    lo_map = lambda b, j, p: (b * n_chunks + j, p)
    hi_map = lambda b, j, p: (b * n_chunks + n_chunks - 1 - j, p)
    return pl.pallas_call(
        _dsa_select_kernel,
        grid=(batch, n_chunks // 2, N_PAIRS),
        in_specs=[pl.BlockSpec((DSA_TQ, LANES), lo_map), pl.BlockSpec((DSA_TQ, LANES), hi_map),
                  pl.BlockSpec((seq, LANES), lambda b, j, p: (b, 0)),
                  pl.BlockSpec((DSA_TQ, LANES), lo_map), pl.BlockSpec((DSA_TQ, LANES), hi_map)],
        out_specs=pl.BlockSpec((2 * DSA_TQ, seq), lambda b, j, p: (b * (n_chunks // 2) + j, 0)),
        out_shape=jax.ShapeDtypeStruct((batch * seq, seq), BF16),
        scratch_shapes=[pltpu.VMEM((DSA_TQ, seq), F32), pltpu.VMEM((DSA_TQ, seq), F32)],
        compiler_params=_cparams("parallel", "arbitrary", "arbitrary"),
        name="dsa_select",
    )(qi, qi, ki2, wi, wi)


def _dsa_attend_kernel(q_ref, k_ref, v_ref, bias_ref, o_ref):
    for i in range(k_ref.shape[0] // DSA_TQ):
        @pl.when(pl.program_id(1) == i)
        def _(i=i):
            n_keys = (i + 1) * DSA_TQ
            bias = bias_ref[:, 0:n_keys].astype(F32)
            bias = jnp.concatenate([bias, bias], axis=0)
            pairs = [slice(u * LANES, (u + 1) * LANES) for u in range(PAIRS_PER_STEP)]
            scores = [_dot_nt(_stack_heads(q_ref[:, pair]), k_ref[0:n_keys, pair]) + bias for pair in pairs]
            _attend_pairs(pairs, scores, v_ref, n_keys, o_ref)


def _dsa_attend(qb, kb, vb, bias, batch, seq):
    n_chunks = seq // DSA_TQ
    q_map = lambda b, c, p: (b * n_chunks + c, p)
    kv_map = lambda b, c, p: (b, p)
    return pl.pallas_call(
        _dsa_attend_kernel,
        grid=(batch, n_chunks, N_PAIRS // PAIRS_PER_STEP),
        in_specs=[pl.BlockSpec((DSA_TQ, STEP_LANES), q_map),
                  pl.BlockSpec((seq, STEP_LANES), kv_map),
                  pl.BlockSpec((seq, STEP_LANES), kv_map),
                  pl.BlockSpec((DSA_TQ, seq), lambda b, c, p: (b * n_chunks + _dsa_bias_block(c, n_chunks), 0))],
        out_specs=pl.BlockSpec((DSA_TQ, STEP_LANES), q_map),
        out_shape=jax.ShapeDtypeStruct((batch * seq, WIDTH), BF16),
        compiler_params=_cparams("parallel", "parallel", "arbitrary"),
        name="dsa_attend",
    )(qb, kb, vb, bias)


MIX_TM = 512
ROUTE_ROWS = 8


def _mix_out_kernel(x_ref, oa_ref, ob_ref, sga_ref, sgb_ref, wa_ref, wb_ref, wo_ref, g_ref, wr_ref, br_ref,
                    x1_ref, xs_ref, route_ref, route_t_ref):
    mixed = (sga_ref[...].astype(F32) * _dot(oa_ref[...], wa_ref[...])
             + sgb_ref[...].astype(F32) * _dot(ob_ref[...], wb_ref[...]))
    x1 = x_ref[...] + _dot(mixed.astype(BF16), wo_ref[...])
    x1_ref[...] = x1
    hn = x1 * lax.rsqrt(jnp.mean(x1 * x1, axis=-1, keepdims=True) + RMS_EPS) * g_ref[...]
    xs_ref[...] = hn

    hi = hn.astype(BF16)
    lo = (hn - hi.astype(F32)).astype(BF16)
    both = _dot(hi, wr_ref[...])
    logits = both[:, :LANES] + both[:, LANES:] + _dot(lo, wr_ref[:, :LANES]) + br_ref[...]
    lane = lax.broadcasted_iota(I32, logits.shape, 1)
    far = jnp.int32(LANES)

    def first_lane_of_max(v, valid):
        top = jnp.max(jnp.where(valid, v, NEG), axis=-1, keepdims=True)
        return top, jnp.min(jnp.where(valid & (v == top), lane, far), axis=-1, keepdims=True)

    is_group = lane < N_GROUPS
    g_max, g_sel = first_lane_of_max(logits, is_group)
    g_w = 1.0 / jnp.sum(jnp.where(is_group, jnp.exp(logits - g_max), 0.0), axis=-1, keepdims=True)
    first = N_GROUPS + g_sel * EXPERTS_PER_GROUP
    in_group = (lane >= first) & (lane < first + EXPERTS_PER_GROUP)
    e_max, _ = first_lane_of_max(logits, in_group)
    e_exp = jnp.where(in_group, jnp.exp(logits - e_max), 0.0)
    prob = e_exp / jnp.sum(e_exp, axis=-1, keepdims=True)
    p1, i1 = first_lane_of_max(prob, in_group)
    p2, i2 = first_lane_of_max(prob, in_group & (lane != i1))
    denom = p1 + p2
    record = jnp.where(lane == 0, (i1 - N_GROUPS).astype(F32),
                       jnp.where(lane == 1, (i2 - N_GROUPS).astype(F32),
                                 jnp.where(lane == 2, g_w * p1 / denom,
                                           jnp.where(lane == 3, g_w * p2 / denom, 0.0))))
    route_ref[...] = record
    route_t_ref[...] = record.T[:ROUTE_ROWS, :]


def _mix_out(x2, oa, ob, sga, sgb, wa, wb, wo, g_ffn, wr, br):
    t = x2.shape[0]
    row = lambda i: (i, 0)
    const = lambda i: (0, 0)
    once = dict(pipeline_mode=pl.Buffered(1))
    return pl.pallas_call(
        _mix_out_kernel,
        grid=(t // MIX_TM,),
        in_specs=[pl.BlockSpec((MIX_TM, D_MODEL), row),
                  pl.BlockSpec((MIX_TM, WIDTH), row), pl.BlockSpec((MIX_TM, WIDTH), row),
                  pl.BlockSpec((MIX_TM, D_MODEL), row), pl.BlockSpec((MIX_TM, D_MODEL), row),
                  pl.BlockSpec((WIDTH, D_MODEL), const, **once), pl.BlockSpec((WIDTH, D_MODEL), const, **once),
                  pl.BlockSpec((D_MODEL, D_MODEL), const, **once), pl.BlockSpec((1, D_MODEL), const),
                  pl.BlockSpec((D_MODEL, 2 * LANES), const, **once),
                  pl.BlockSpec((1, LANES), const)],
        out_specs=[pl.BlockSpec((MIX_TM, D_MODEL), row), pl.BlockSpec((MIX_TM, D_MODEL), row),
                   pl.BlockSpec((MIX_TM, LANES), row), pl.BlockSpec((ROUTE_ROWS, MIX_TM), lambda i: (0, i))],
        out_shape=[jax.ShapeDtypeStruct((t, D_MODEL), F32), jax.ShapeDtypeStruct((t, D_MODEL), F32),
                   jax.ShapeDtypeStruct((t, LANES), F32), jax.ShapeDtypeStruct((ROUTE_ROWS, t), F32)],
        compiler_params=_cparams("parallel"),
        name="mix_out",
    )(x2, oa, ob, sga, sgb, wa, wb, wo, g_ffn, wr, br)


PLAN_TM = 512


def _moe_plan_kernel(route_t_ref, dest_ref, counts_ref, count_ref, start_ref):
    phase = pl.program_id(0)
    step = pl.program_id(1)
    expert = lax.broadcasted_iota(I32, (LANES, PLAN_TM), 0)
    e1 = route_t_ref[0:1, :].astype(I32)
    e2 = route_t_ref[1:2, :].astype(I32)
    hot1 = expert == e1
    hot2 = expert == e2
    hot = jnp.where(hot1 | hot2, 1.0, 0.0)

    @pl.when((phase == 0) & (step == 0))
    def _():
        count_ref[...] = jnp.zeros_like(count_ref)

    @pl.when(phase == 0)
    def _():
        count_ref[...] += jnp.sum(hot, axis=-1, keepdims=True)
        dest_ref[...] = jnp.zeros_like(dest_ref)

    @pl.when((phase == 1) & (step == 0))
    def _():
        counts = jnp.broadcast_to(count_ref[...], (LANES, LANES))
        counts_ref[...] = counts
        padded = jnp.ceil(counts / DISPATCH_BLOCK) * DISPATCH_BLOCK
        sub = lax.broadcasted_iota(I32, (LANES, LANES), 0)
        ends = padded
        shift = 1
        while shift < LANES:
            ends = ends + jnp.where(sub >= shift, pltpu.roll(ends, shift, 0), 0.0)
            shift *= 2
        start_ref[...] = (ends - padded)[:, 0:1]
        count_ref[...] = jnp.zeros_like(count_ref)

    @pl.when(phase == 1)
    def _():
        tok_r = lax.broadcasted_iota(I32, (PLAN_TM, PLAN_TM), 0)
        tok_c = lax.broadcasted_iota(I32, (PLAN_TM, PLAN_TM), 1)
        earlier = jnp.where(tok_r < tok_c, 1.0, 0.0).astype(BF16)
        slot = start_ref[...] + count_ref[...] + _dot(hot.astype(BF16), earlier)
        d1 = jnp.sum(jnp.where(hot1, slot, 0.0), axis=0, keepdims=True)
        d2 = jnp.sum(jnp.where(hot2, slot, 0.0), axis=0, keepdims=True)
        sub = lax.broadcasted_iota(I32, (ROUTE_ROWS, PLAN_TM), 0)
        dest_ref[...] = jnp.where(sub == 0, d1, jnp.where(sub == 1, d2, 0.0)).astype(I32)
        count_ref[...] += jnp.sum(hot, axis=-1, keepdims=True)


def _moe_plan(route_t):
    t = route_t.shape[1]
    return pl.pallas_call(
        _moe_plan_kernel,
        grid=(2, t // PLAN_TM),
        in_specs=[pl.BlockSpec((ROUTE_ROWS, PLAN_TM), lambda ph, i: (0, i))],
        out_specs=[pl.BlockSpec((ROUTE_ROWS, PLAN_TM), lambda ph, i: (0, i * ph)),
                   pl.BlockSpec((LANES, LANES), lambda ph, i: (0, 0))],
        out_shape=[jax.ShapeDtypeStruct((ROUTE_ROWS, t), I32), jax.ShapeDtypeStruct((LANES, LANES), F32)],
        scratch_shapes=[pltpu.VMEM((LANES, 1), F32), pltpu.VMEM((LANES, 1), F32)],
        compiler_params=_cparams("arbitrary", "arbitrary"),
        name="moe_plan",
    )(route_t)


DISP_TM = 1024
DMA_UNROLL = 8


def _moe_dispatch_kernel(d1_ref, d2_ref, xs_ref, zeros_ref, out_ref, sem):
    del zeros_ref

    def row_copy(r, dest):
        return pltpu.make_async_copy(xs_ref.at[pl.ds(r, 1), :], out_ref.at[pl.ds(dest, 1), :], sem)

    def issue(r, carry):
        row_copy(r, d1_ref[r]).start(priority=0)
        row_copy(r, d2_ref[r]).start(priority=1)
        return carry
    lax.fori_loop(0, DISP_TM, issue, 0, unroll=DMA_UNROLL)

    for _ in range(2):
        pltpu.make_async_copy(xs_ref, out_ref.at[pl.ds(0, DISP_TM), :], sem).wait()


def _moe_dispatch(d1, d2, xs, n_rows):
    t = xs.shape[0]
    smem = lambda: pl.BlockSpec((DISP_TM,), lambda i: (i,), memory_space=pltpu.SMEM)
    return pl.pallas_call(
        _moe_dispatch_kernel,
        grid=(t // DISP_TM,),
        in_specs=[smem(), smem(), pl.BlockSpec((DISP_TM, D_MODEL), lambda i: (i, 0)),
                  pl.BlockSpec(memory_space=pl.ANY)],
        out_specs=pl.BlockSpec(memory_space=pl.ANY),
        out_shape=jax.ShapeDtypeStruct((n_rows, D_MODEL), F32),
        scratch_shapes=[pltpu.SemaphoreType.DMA(())],
        input_output_aliases={3: 0},
        compiler_params=_cparams("arbitrary"),
        name="moe_dispatch",
    )(d1, d2, xs, jnp.zeros((n_rows, D_MODEL), F32))


def _moe_expert_kernel(block_expert_ref, n_used_ref, xs_ref, w1_ref, w3_ref, w2_ref, ys_ref):
    del block_expert_ref
    live = pl.program_id(0) < n_used_ref[0]

    @pl.when(live)
    def _():
        x = xs_ref[...].astype(BF16)
        h1 = _dot(x, w1_ref[0])
        h3 = _dot(x, w3_ref[0])
        hid = h1 / (1.0 + jnp.exp(-h1)) * h3
        ys_ref[...] = _dot(hid.astype(BF16), w2_ref[0])

    @pl.when(jnp.logical_not(live))
    def _():
        ys_ref[...] = jnp.zeros_like(ys_ref)


def _moe_experts(block_expert, n_used, xs_sorted, w1, w3, w2):
    n_rows = xs_sorted.shape[0]
    grid_spec = pltpu.PrefetchScalarGridSpec(
        num_scalar_prefetch=2,
        grid=(n_rows // DISPATCH_BLOCK,),
        in_specs=[pl.BlockSpec((DISPATCH_BLOCK, D_MODEL), lambda j, be, nu: (j, 0)),
                  pl.BlockSpec((1, D_MODEL, EXPERT_FF), lambda j, be, nu: (be[j], 0, 0)),
                  pl.BlockSpec((1, D_MODEL, EXPERT_FF), lambda j, be, nu: (be[j], 0, 0)),
                  pl.BlockSpec((1, EXPERT_FF, D_MODEL), lambda j, be, nu: (be[j], 0, 0))],
        out_specs=pl.BlockSpec((DISPATCH_BLOCK, D_MODEL), lambda j, be, nu: (j, 0)))
    return pl.pallas_call(
        _moe_expert_kernel,
        grid_spec=grid_spec,
        out_shape=jax.ShapeDtypeStruct((n_rows, D_MODEL), F32),
        compiler_params=_cparams("arbitrary"),
        name="moe_experts",
    )(block_expert, n_used, xs_sorted, w1, w3, w2)


COMB_TM = 512


def _moe_combine_kernel(d1_ref, d2_ref, x1_ref, route_ref, g_ref, ys_ref, out_ref, y1_ref, y2_ref, sem):
    def row_copy(src, r, buf):
        return pltpu.make_async_copy(ys_ref.at[pl.ds(src, 1), :], buf.at[pl.ds(r, 1), :], sem)

    def issue(r, carry):
        row_copy(d1_ref[r], r, y1_ref).start(priority=0)
        row_copy(d2_ref[r], r, y2_ref).start(priority=1)
        return carry
    lax.fori_loop(0, COMB_TM, issue, 0, unroll=DMA_UNROLL)

    for buf in (y1_ref, y2_ref):
        pltpu.make_async_copy(ys_ref.at[pl.ds(0, COMB_TM), :], buf, sem).wait()

    route = route_ref[...]
    x2 = x1_ref[...] + (route[:, 2:3] * y1_ref[...] + route[:, 3:4] * y2_ref[...])
    out_ref[...] = x2 * lax.rsqrt(jnp.mean(x2 * x2, axis=-1, keepdims=True) + RMS_EPS) * g_ref[...]


def _moe_combine(d1, d2, x1, route, g_final, ys):
    t = x1.shape[0]
    smem = lambda: pl.BlockSpec((COMB_TM,), lambda i: (i,), memory_space=pltpu.SMEM)
    row = lambda i: (i, 0)
    return pl.pallas_call(
        _moe_combine_kernel,
        grid=(t // COMB_TM,),
        in_specs=[smem(), smem(), pl.BlockSpec((COMB_TM, D_MODEL), row), pl.BlockSpec((COMB_TM, LANES), row),
                  pl.BlockSpec((1, D_MODEL), lambda i: (0, 0)), pl.BlockSpec(memory_space=pl.ANY)],
        out_specs=pl.BlockSpec((COMB_TM, D_MODEL), row),
        out_shape=jax.ShapeDtypeStruct((t, D_MODEL), F32),
        scratch_shapes=[pltpu.VMEM((COMB_TM, D_MODEL), F32), pltpu.VMEM((COMB_TM, D_MODEL), F32),
                        pltpu.SemaphoreType.DMA(())],
        compiler_params=_cparams("arbitrary"),
        name="moe_combine",
    )(d1, d2, x1, route, g_final, ys)


def _hier_moe_tail(x1, xs, route, route_t, w1, w3, w2, g_final):
    t = x1.shape[0]
    dest, counts = _moe_plan(route_t)
    counts = counts[:N_EXPERTS, 0].astype(I32)
    padded = (counts + DISPATCH_BLOCK - 1) // DISPATCH_BLOCK * DISPATCH_BLOCK
    ends = jnp.cumsum(padded)
    n_blocks = (t * 2) // DISPATCH_BLOCK + N_EXPERTS
    block_start = jnp.arange(n_blocks, dtype=I32) * DISPATCH_BLOCK
    block_expert = jnp.minimum(jnp.sum((ends[None, :] <= block_start[:, None]).astype(I32), axis=1), N_EXPERTS - 1)
    n_used = (ends[-1:] // DISPATCH_BLOCK).astype(I32)
    d1, d2 = dest[0], dest[1]
    xs_sorted = _moe_dispatch(d1, d2, xs, n_blocks * DISPATCH_BLOCK)
    ys = _moe_experts(block_expert, n_used, xs_sorted, w1, w3, w2)
    return _moe_combine(d1, d2, x1, route, g_final, ys)


def kernel(x, g_mix, w_in, w_proj_a, w_proj_b, w_out, g_ffn, w_group, b_group, w_expert, b_expert, w1, w3, w2, g_final):
    batch, seq, d = x.shape
    assert d == D_MODEL and g_mix.shape[0] == 1, "one layer of width D_MODEL"
    x2 = x.reshape(batch * seq, d)
    cos_t, sin_t = _rope_tables(seq)
    qa, ka, va, qb, kb, vb, qi, ki2, wi, sga, sgb, kmean = _in_proj(
        x2, g_mix[0][None, :], _arrange_w_in(w_in[0]), cos_t, sin_t, seq)
    selb = _moba_select(qa, kmean, batch, seq)
    oa = _moba_attn(qa, ka, va, selb, batch, seq)
    ob = _dsa_attend(qb, kb, vb, _dsa_select(qi, ki2, wi, batch, seq), batch, seq)
    wr, br = _router_params(w_group[0], b_group[0], w_expert[0], b_expert[0])
    x1, xs, route, route_t = _mix_out(
        x2, oa, ob, sga, sgb, w_proj_a[0].astype(BF16), w_proj_b[0].astype(BF16), w_out[0].astype(BF16),
        g_ffn[0][None, :], wr, br)
    out = _hier_moe_tail(x1, xs, route, route_t, w1[0].astype(BF16), w3[0].astype(BF16), w2[0].astype(BF16),
                         g_final[None, :])
    return out.reshape(batch, seq, d)


def _router_params(w_group, b_group, w_expert, b_expert):
    w = jnp.pad(jnp.concatenate([w_group, w_expert], axis=1), ((0, 0), (0, LANES - N_GROUPS - N_EXPERTS)))
    b = jnp.pad(jnp.concatenate([b_group, b_expert]), (0, LANES - N_GROUPS - N_EXPERTS))[None, :]
    w_hi = w.astype(BF16)
    w_lo = (w - w_hi.astype(F32)).astype(BF16)
    return jnp.concatenate([w_hi, w_lo], axis=1), b
```

```python
import functools

import jax
import jax.numpy as jnp
from jax import lax
from jax.experimental import pallas as pl
from jax.experimental.pallas import tpu as pltpu

F32 = jnp.float32
BF16 = jnp.bfloat16
I32 = jnp.int32

D_MODEL = 1024
HEAD_DIM = 64
N_HEADS = 8
WIDTH = N_HEADS * HEAD_DIM
N_PAIRS = N_HEADS // 2
MOBA_BLOCK = 256
MOBA_TOPK = 3
DSA_TOPK = 256
IDX_SCALE = float(WIDTH) ** -0.5
ATTN_SCALE = float(HEAD_DIM) ** -0.5
N_GROUPS = 4
EXPERTS_PER_GROUP = 8
N_EXPERTS = N_GROUPS * EXPERTS_PER_GROUP
EXPERT_FF = 512
DISPATCH_BLOCK = 512
ROPE_THETA = 10000.0
RMS_EPS = 1e-6
NEG = -1e30

LANES = 128
VMEM_LIMIT = 56 * 1024 * 1024

C_QA, C_KA, C_VA, C_QB, C_KB, C_VB, C_QI = (i * WIDTH for i in range(7))
C_KI = 7 * WIDTH
C_WI = C_KI + LANES
C_GA = C_WI + WIDTH
C_GB = C_GA + D_MODEL
IN_COLS_PADDED = C_GB + D_MODEL


def _cparams(*semantics):
    return pltpu.CompilerParams(dimension_semantics=semantics, vmem_limit_bytes=VMEM_LIMIT)


def _dot_nt(a, b):
    return lax.dot_general(a, b, (((1,), (1,)), ((), ())), preferred_element_type=F32)


def _dot(a, b):
    return jnp.dot(a, b, preferred_element_type=F32)


IN_TM = 512


def _in_proj_kernel(x_ref, g_ref, w_ref, cos_ref, sin_ref,
                    qa_ref, ka_ref, va_ref, qb_ref, kb_ref, vb_ref, qi_ref, ki_ref, wi_ref,
                    sga_ref, sgb_ref, kmean_ref):
    x = x_ref[...]
    h = x * lax.rsqrt(jnp.mean(x * x, axis=-1, keepdims=True) + RMS_EPS) * g_ref[...]
    hb = h.astype(BF16)
    cos = cos_ref[...]
    sin = sin_ref[...]
    upper_half = (lax.broadcasted_iota(I32, (IN_TM, LANES), 1) & (HEAD_DIM // 2)) != 0

    def rope(v):
        partner = jnp.where(upper_half, pltpu.roll(v, HEAD_DIM // 2, 1), pltpu.roll(v, LANES - HEAD_DIM // 2, 1))
        return v * cos + partner * sin

    def proj(c0, width):
        return _dot(hb, w_ref[:, c0:c0 + width])

    def store_heads(ref, c0, rotary):
        r = proj(c0, WIDTH)
        for j in range(WIDTH // LANES):
            v = r[:, j * LANES:(j + 1) * LANES]
            ref[:, j * LANES:(j + 1) * LANES] = (rope(v) if rotary else v).astype(ref.dtype)

    store_heads(qa_ref, C_QA, True)
    store_heads(va_ref, C_VA, False)
    store_heads(qb_ref, C_QB, True)
    store_heads(kb_ref, C_KB, True)
    store_heads(vb_ref, C_VB, False)
    store_heads(qi_ref, C_QI, True)

    r = proj(C_KA, WIDTH)
    for j in range(WIDTH // LANES):
        v = rope(r[:, j * LANES:(j + 1) * LANES])
        ka_ref[:, j * LANES:(j + 1) * LANES] = v.astype(BF16)
        for blk in range(IN_TM // MOBA_BLOCK):
            kmean_ref[blk, :, j * LANES:(j + 1) * LANES] = jnp.mean(
                v[blk * MOBA_BLOCK:(blk + 1) * MOBA_BLOCK], axis=0, keepdims=True)

    ki_ref[...] = rope(proj(C_KI, LANES)).astype(BF16)
    wi_ref[...] = proj(C_WI, WIDTH) * IDX_SCALE
    sga_ref[...] = (1.0 / (1.0 + jnp.exp(-proj(C_GA, D_MODEL)))).astype(BF16)
    sgb_ref[...] = (1.0 / (1.0 + jnp.exp(-proj(C_GB, D_MODEL)))).astype(BF16)


def _in_proj(x2, g_mix, w_r, cos_t, sin_t, seq):
    t = x2.shape[0]
    n_tiles = t // IN_TM
    tiles_per_seq = seq // IN_TM
    row = lambda i: (i, 0)
    const = lambda i: (0, 0)
    act = lambda w, dt: jax.ShapeDtypeStruct((t, w), dt)
    out_shape = ([act(WIDTH, BF16)] * 7 + [act(LANES, BF16), act(WIDTH, F32), act(D_MODEL, BF16), act(D_MODEL, BF16),
                                           jax.ShapeDtypeStruct((t // MOBA_BLOCK, 1, WIDTH), F32)])
    out_specs = ([pl.BlockSpec((IN_TM, WIDTH), row)] * 7
                 + [pl.BlockSpec((IN_TM, LANES), row), pl.BlockSpec((IN_TM, WIDTH), row),
                    pl.BlockSpec((IN_TM, D_MODEL), row), pl.BlockSpec((IN_TM, D_MODEL), row),
                    pl.BlockSpec((IN_TM // MOBA_BLOCK, 1, WIDTH), lambda i: (i, 0, 0))])
    return pl.pallas_call(
        _in_proj_kernel,
        grid=(n_tiles,),
        in_specs=[pl.BlockSpec((IN_TM, D_MODEL), row),
                  pl.BlockSpec((1, D_MODEL), const),
                  pl.BlockSpec((D_MODEL, IN_COLS_PADDED), const, pipeline_mode=pl.Buffered(1)),
                  pl.BlockSpec((IN_TM, LANES), lambda i: (i % tiles_per_seq, 0)),
                  pl.BlockSpec((IN_TM, LANES), lambda i: (i % tiles_per_seq, 0))],
        out_specs=out_specs,
        out_shape=out_shape,
        compiler_params=_cparams("parallel"),
        name="in_proj",
    )(x2, g_mix, w_r, cos_t, sin_t)


def _moba_select_kernel(qa_ref, km_ref, selb_ref):
    n_blocks = km_ref.shape[1]
    own = lax.broadcasted_iota(I32, (qa_ref.shape[0], LANES), 0) // MOBA_BLOCK
    km = km_ref[0]
    col_head = lax.broadcasted_iota(I32, (N_HEADS, WIDTH), 1) // HEAD_DIM
    head_mask = col_head == lax.broadcasted_iota(I32, (N_HEADS, WIDTH), 0)
    rows = [jnp.where(head_mask, km[n:n + 1, :], 0.0) for n in range(n_blocks)]
    rows.append(jnp.zeros((LANES - n_blocks * N_HEADS, WIDTH), F32))
    km_t = jnp.concatenate(rows, axis=0).astype(BF16)
    gate = _dot_nt(qa_ref[...], km_t)
    lane = lax.broadcasted_iota(I32, gate.shape, 1)
    past = (lane // N_HEADS) < own
    g = jnp.where(past, gate, NEG)
    rank = jnp.zeros(gate.shape, I32)
    for r in range(1, n_blocks):
        later = pltpu.roll(g, LANES - N_HEADS * r, 1)
        earlier = pltpu.roll(g, N_HEADS * r, 1)
        rank = rank + (later > g).astype(I32) + (earlier >= g).astype(I32)
    chosen = jnp.where(past & (rank < MOBA_TOPK), 1.0, 0.0).astype(BF16)
    src = lax.broadcasted_iota(I32, (LANES, N_PAIRS * LANES), 0)
    dst = lax.broadcasted_iota(I32, (LANES, N_PAIRS * LANES), 1)
    src_n, src_h = src // N_HEADS, src % N_HEADS
    dst_pair, dst_w = dst // LANES, dst % LANES
    expand = ((src_n < n_blocks) & (dst_w < 2 * N_HEADS) & (dst_w % N_HEADS == src_n)
              & (dst_pair * 2 + dst_w // N_HEADS == src_h))
    hit = _dot(chosen, jnp.where(expand, 1.0, 0.0).astype(BF16))
    selb_ref[...] = jnp.where(hit > 0.5, 0.0, NEG).astype(selb_ref.dtype)


def _moba_select(qa, kmean, batch, seq):
    n_blocks = seq // MOBA_BLOCK
    assert n_blocks * N_HEADS <= LANES and n_blocks <= N_HEADS
    t = batch * seq
    return pl.pallas_call(
        _moba_select_kernel,
        grid=(batch,),
        in_specs=[pl.BlockSpec((seq, WIDTH), lambda b: (b, 0)),
                  pl.BlockSpec((1, n_blocks, WIDTH), lambda b: (b, 0, 0))],
        out_specs=pl.BlockSpec((seq, N_PAIRS * LANES), lambda b: (b, 0)),
        out_shape=jax.ShapeDtypeStruct((t, N_PAIRS * LANES), BF16),
        compiler_params=_cparams("parallel"),
        name="moba_select",
    )(qa, kmean.reshape(batch, n_blocks, WIDTH))


def _stack_heads(q2):
    lane = lax.broadcasted_iota(I32, q2.shape, 1)
    zero = jnp.zeros_like(q2)
    return jnp.concatenate([jnp.where(lane < HEAD_DIM, q2, zero), jnp.where(lane >= HEAD_DIM, q2, zero)], axis=0)


def _unstack_heads(o):
    rows = o.shape[0] // 2
    lane = lax.broadcasted_iota(I32, (rows, LANES), 1)
    return jnp.where(lane < HEAD_DIM, o[:rows], o[rows:])


def _attend_pairs(pairs, scores, v_ref, n_keys, o_ref):
    tops = [jnp.max(s, axis=-1, keepdims=True) for s in scores]
    probs = [jnp.exp(s - m) for s, m in zip(scores, tops)]
    outs = [_dot(p.astype(BF16), v_ref[0:n_keys, pair]) for pair, p in zip(pairs, probs)]
    sums = [jnp.sum(p, axis=-1, keepdims=True) for p in probs]
    for pair, o, l in zip(pairs, outs, sums):
        o_ref[:, pair] = _unstack_heads(o / l).astype(o_ref.dtype)


PAIRS_PER_STEP = 2
STEP_LANES = PAIRS_PER_STEP * LANES


def _moba_attn_kernel(q_ref, k_ref, v_ref, selb_ref, o_ref):
    own = pl.program_id(2)
    n_blocks = k_ref.shape[0] // MOBA_BLOCK
    q_pos = lax.broadcasted_iota(I32, (2 * MOBA_BLOCK, MOBA_BLOCK), 0) % MOBA_BLOCK
    causal = lax.broadcasted_iota(I32, (2 * MOBA_BLOCK, MOBA_BLOCK), 1) <= q_pos
    for i in range(n_blocks):
        @pl.when(own == i)
        def _(i=i):
            n_keys = (i + 1) * MOBA_BLOCK
            key_blk = lax.broadcasted_iota(I32, (n_keys, LANES), 0) // MOBA_BLOCK
            key_lane = lax.broadcasted_iota(I32, (n_keys, LANES), 1)
            indicator = jnp.where((key_lane < 2 * N_HEADS) & (key_lane % N_HEADS == key_blk) & (key_blk < i),
                                  1.0, 0.0).astype(BF16)
            q_lane = lax.broadcasted_iota(I32, (MOBA_BLOCK, LANES), 1)
            pairs = [slice(u * LANES, (u + 1) * LANES) for u in range(PAIRS_PER_STEP)]
            scores = []
            for pair in pairs:
                selb = selb_ref[:, pair]
                zero = jnp.zeros_like(selb)
                bias_rows = jnp.concatenate([jnp.where(q_lane < N_HEADS, selb, zero),
                                             jnp.where((q_lane >= N_HEADS) & (q_lane < 2 * N_HEADS), selb, zero)], axis=0)
                s = _dot_nt(jnp.concatenate([_stack_heads(q_ref[:, pair]), bias_rows], axis=1),
                            jnp.concatenate([k_ref[0:n_keys, pair], indicator], axis=1))
                own_s = jnp.where(causal, s[:, i * MOBA_BLOCK:], NEG)
                scores.append(own_s if i == 0 else jnp.concatenate([s[:, :i * MOBA_BLOCK], own_s], axis=1))
            _attend_pairs(pairs, scores, v_ref, n_keys, o_ref)


def _moba_attn(qa, ka, va, selb, batch, seq):
    n_blocks = seq // MOBA_BLOCK
    t = batch * seq
    q_map = lambda b, p, i: (b * n_blocks + i, p)
    kv_map = lambda b, p, i: (b, p)
    return pl.pallas_call(
        _moba_attn_kernel,
        grid=(batch, N_PAIRS // PAIRS_PER_STEP, n_blocks),
        in_specs=[pl.BlockSpec((MOBA_BLOCK, STEP_LANES), q_map),
                  pl.BlockSpec((seq, STEP_LANES), kv_map),
                  pl.BlockSpec((seq, STEP_LANES), kv_map),
                  pl.BlockSpec((MOBA_BLOCK, STEP_LANES), q_map)],
        out_specs=pl.BlockSpec((MOBA_BLOCK, STEP_LANES), q_map),
        out_shape=jax.ShapeDtypeStruct((t, WIDTH), BF16),
        compiler_params=_cparams("parallel", "parallel", "arbitrary"),
        name="moba_attn",
    )(qa, ka, va, selb)


def _rope_tables(seq):
    inv = jnp.power(ROPE_THETA, -jnp.arange(0, HEAD_DIM, 2, dtype=F32) / HEAD_DIM)
    ang = jnp.arange(seq, dtype=F32)[:, None] * inv[None, :]
    cos, sin = jnp.cos(ang), jnp.sin(ang)
    reps = LANES // HEAD_DIM
    return jnp.tile(jnp.concatenate([cos, cos], axis=-1), (1, reps)), jnp.tile(jnp.concatenate([-sin, sin], axis=-1), (1, reps))


def _arrange_w_in(w_in):
    sizes = (WIDTH,) * 7 + (HEAD_DIM, N_HEADS, D_MODEL, D_MODEL)
    offs = [0]
    for s in sizes:
        offs.append(offs[-1] + s)
    seg = [w_in[:, offs[i]:offs[i + 1]] for i in range(len(sizes))]
    qa, ka, va, qb, kb, vb, qi, ki, wi, ga, gb = seg
    wi_pairs = jnp.pad(wi.reshape(-1, N_PAIRS, 2), ((0, 0), (0, 0), (0, LANES - 2))).reshape(-1, WIDTH)
    w = jnp.concatenate([qa * ATTN_SCALE, ka, va, qb * ATTN_SCALE, kb, vb, qi, ki, ki, wi_pairs, ga, gb], axis=1)
    return w.astype(BF16)


DSA_TQ = 256
DSA_ROWS = 64
INT_MIN = -(2 ** 31)


def _count_lanes(hit):
    acc = hit[:, :LANES]
    for j in range(1, hit.shape[1] // LANES):
        acc = acc + hit[:, j * LANES:(j + 1) * LANES]
    return jnp.sum(acc, axis=-1, keepdims=True)


def _dsa_score_pair(jobs, ki_ref):
    logits = [_dot_nt(_stack_heads(qi_ref[...]), ki_ref[0:(i + 1) * DSA_TQ, :]) for i, qi_ref, _, _ in jobs]
    parts = []
    for (_, _, wi_ref, _), logit in zip(jobs, logits):
        wi = wi_ref[...]
        parts.append(wi[:, 0:1] * jnp.maximum(logit[:DSA_TQ], 0.0) + wi[:, 1:2] * jnp.maximum(logit[DSA_TQ:], 0.0))
    pair = pl.program_id(2)

    @pl.when(pair == 0)
    def _():
        for (i, _, _, score_ref), part in zip(jobs, parts):
            score_ref[:, 0:(i + 1) * DSA_TQ] = part

    @pl.when(pair > 0)
    def _():
        for (i, _, _, score_ref), part in zip(jobs, parts):
            score_ref[:, 0:(i + 1) * DSA_TQ] += part


def _dsa_thresholds(jobs):
    def as_float(code):
        return pltpu.bitcast(code ^ ((code >> 31) & 0x7FFFFFFF), F32)

    n_rg = DSA_TQ // DSA_ROWS
    units = [(i, score_ref, rg) for i, score_ref in jobs for rg in range(n_rg)]

    def step(unit, t_old, cand):
        i, score_ref, rg = unit
        k = score_ref[rg * DSA_ROWS:(rg + 1) * DSA_ROWS, 0:(i + 1) * DSA_TQ]
        n = _count_lanes(jnp.where(k >= as_float(cand), 1.0, 0.0))
        return jnp.where(n >= DSA_TOPK, cand, t_old)

    def bit_body(ib, ts):
        bit = jnp.left_shift(jnp.int32(1), 30 - ib)
        return tuple(step(u, t, t | bit) for u, t in zip(units, ts))

    ts = tuple(step(u, jnp.full((DSA_ROWS, 1), INT_MIN, I32), jnp.zeros((DSA_ROWS, 1), I32)) for u in units)
    ts = lax.fori_loop(0, 31, bit_body, ts)
    out = []
    for j in range(len(jobs)):
        code = jnp.concatenate(ts[j * n_rg:(j + 1) * n_rg], axis=0)
        out.append(jnp.where(code == INT_MIN, -jnp.inf, as_float(code)))
    return out


def _dsa_write_bias(i, score_ref, thr, bias_ref, first_row):
    n_keys = (i + 1) * DSA_TQ
    rows = slice(first_row, first_row + DSA_TQ)
    row = lax.broadcasted_iota(I32, (DSA_TQ, DSA_TQ), 0)
    col = lax.broadcasted_iota(I32, (DSA_TQ, DSA_TQ), 1)
    keys = score_ref[:, 0:n_keys]
    need = float(DSA_TOPK) - _count_lanes(jnp.where(keys > thr, 1.0, 0.0))
    strictly_before = jnp.where(row < col, 1.0, 0.0).astype(BF16)
    ties_seen = jnp.zeros((DSA_TQ, 1), F32)
    for t in range(i + 1):
        k = keys[:, t * DSA_TQ:(t + 1) * DSA_TQ]
        tie = jnp.where(k == thr, 1.0, 0.0)
        ties_before = ties_seen + _dot(tie.astype(BF16), strictly_before)
        chosen = (k > thr) | ((k == thr) & (ties_before < need))
        if t == i:
            chosen = chosen & (col <= row)
        bias_ref[rows, t * DSA_TQ:(t + 1) * DSA_TQ] = jnp.where(chosen, 0.0, NEG).astype(bias_ref.dtype)
        ties_seen = ties_seen + jnp.sum(tie, axis=-1, keepdims=True)
    if n_keys < bias_ref.shape[1]:
        bias_ref[rows, n_keys:] = jnp.full((DSA_TQ, bias_ref.shape[1] - n_keys), NEG, bias_ref.dtype)


def _dsa_select_kernel(qi_lo_ref, qi_hi_ref, ki_ref, wi_lo_ref, wi_hi_ref, bias_ref, score_lo_ref, score_hi_ref):
    n_chunks = ki_ref.shape[0] // DSA_TQ
    for j in range(n_chunks // 2):
        @pl.when(pl.program_id(1) == j)
        def _(j=j):
            lo, hi = j, n_chunks - 1 - j
            _dsa_score_pair([(lo, qi_lo_ref, wi_lo_ref, score_lo_ref), (hi, qi_hi_ref, wi_hi_ref, score_hi_ref)], ki_ref)

            @pl.when(pl.program_id(2) == N_PAIRS - 1)
            def _():
                row = lax.broadcasted_iota(I32, (DSA_TQ, DSA_TQ), 0)
                col = lax.broadcasted_iota(I32, (DSA_TQ, DSA_TQ), 1)
                jobs = [(lo, score_lo_ref), (hi, score_hi_ref)]
                for i, score_ref in jobs:
                    diag = slice(i * DSA_TQ, (i + 1) * DSA_TQ)
                    score_ref[:, diag] = jnp.where(col <= row, score_ref[:, diag], NEG)
                for k, ((i, score_ref), thr) in enumerate(zip(jobs, _dsa_thresholds(jobs))):
                    _dsa_write_bias(i, score_ref, thr, bias_ref, k * DSA_TQ)


def _dsa_bias_block(c, n_chunks):
    return jnp.where(c < n_chunks // 2, 2 * c, 2 * (n_chunks - 1 - c) + 1)


def _dsa_select(qi, ki2, wi, batch, seq):
    assert seq % (2 * DSA_TQ) == 0 and min(DSA_TOPK, seq // 4) == DSA_TOPK
    n_chunks = seq // DSA_TQ
    lo_map = lambda b, j, p: (b * n_chunks + j, p)
    hi_map = lambda b, j, p: (b * n_chunks + n_chunks - 1 - j, p)
    return pl.pallas_call(
        _dsa_select_kernel,
        grid=(batch, n_chunks // 2, N_PAIRS),
        in_specs=[pl.BlockSpec((DSA_TQ, LANES), lo_map), pl.BlockSpec((DSA_TQ, LANES), hi_map),
                  pl.BlockSpec((seq, LANES), lambda b, j, p: (b, 0)),
                  pl.BlockSpec((DSA_TQ, LANES), lo_map), pl.BlockSpec((DSA_TQ, LANES), hi_map)],
        out_specs=pl.BlockSpec((2 * DSA_TQ, seq), lambda b, j, p: (b * (n_chunks // 2) + j, 0)),
        out_shape=jax.ShapeDtypeStruct((batch * seq, seq), BF16),
        scratch_shapes=[pltpu.VMEM((DSA_TQ, seq), F32), pltpu.VMEM((DSA_TQ, seq), F32)],
        compiler_params=_cparams("parallel", "arbitrary", "arbitrary"),
        name="dsa_select",
    )(qi, qi, ki2, wi, wi)


def _dsa_attend_kernel(q_ref, k_ref, v_ref, bias_ref, o_ref):
    for i in range(k_ref.shape[0] // DSA_TQ):
        @pl.when(pl.program_id(1) == i)
        def _(i=i):
            n_keys = (i + 1) * DSA_TQ
            bias = bias_ref[:, 0:n_keys].astype(F32)
            bias = jnp.concatenate([bias, bias], axis=0)
            pairs = [slice(u * LANES, (u + 1) * LANES) for u in range(PAIRS_PER_STEP)]
            scores = [_dot_nt(_stack_heads(q_ref[:, pair]), k_ref[0:n_keys, pair]) + bias for pair in pairs]
            _attend_pairs(pairs, scores, v_ref, n_keys, o_ref)


def _dsa_attend(qb, kb, vb, bias, batch, seq):
    n_chunks = seq // DSA_TQ
    q_map = lambda b, c, p: (b * n_chunks + c, p)
    kv_map = lambda b, c, p: (b, p)
    return pl.pallas_call(
        _dsa_attend_kernel,
        grid=(batch, n_chunks, N_PAIRS // PAIRS_PER_STEP),
        in_specs=[pl.BlockSpec((DSA_TQ, STEP_LANES), q_map),
                  pl.BlockSpec((seq, STEP_LANES), kv_map),
                  pl.BlockSpec((seq, STEP_LANES), kv_map),
                  pl.BlockSpec((DSA_TQ, seq), lambda b, c, p: (b * n_chunks + _dsa_bias_block(c, n_chunks), 0))],
        out_specs=pl.BlockSpec((DSA_TQ, STEP_LANES), q_map),
        out_shape=jax.ShapeDtypeStruct((batch * seq, WIDTH), BF16),
        compiler_params=_cparams("parallel", "parallel", "arbitrary"),
        name="dsa_attend",
    )(qb, kb, vb, bias)


MIX_TM = 512
ROUTE_ROWS = 8


def _mix_out_kernel(x_ref, oa_ref, ob_ref, sga_ref, sgb_ref, wa_ref, wb_ref, wo_ref, g_ref, wr_ref, br_ref,
                    x1_ref, xs_ref, route_ref, route_t_ref):
    mixed = (sga_ref[...].astype(F32) * _dot(oa_ref[...], wa_ref[...])
             + sgb_ref[...].astype(F32) * _dot(ob_ref[...], wb_ref[...]))
    x1 = x_ref[...] + _dot(mixed.astype(BF16), wo_ref[...])
    x1_ref[...] = x1
    hn = x1 * lax.rsqrt(jnp.mean(x1 * x1, axis=-1, keepdims=True) + RMS_EPS) * g_ref[...]
    xs_ref[...] = hn

    hi = hn.astype(BF16)
    lo = (hn - hi.astype(F32)).astype(BF16)
    both = _dot(hi, wr_ref[...])
    logits = both[:, :LANES] + both[:, LANES:] + _dot(lo, wr_ref[:, :LANES]) + br_ref[...]
    lane = lax.broadcasted_iota(I32, logits.shape, 1)
    far = jnp.int32(LANES)

    def first_lane_of_max(v, valid):
        top = jnp.max(jnp.where(valid, v, NEG), axis=-1, keepdims=True)
        return top, jnp.min(jnp.where(valid & (v == top), lane, far), axis=-1, keepdims=True)

    is_group = lane < N_GROUPS
    g_max, g_sel = first_lane_of_max(logits, is_group)
    g_w = 1.0 / jnp.sum(jnp.where(is_group, jnp.exp(logits - g_max), 0.0), axis=-1, keepdims=True)
    first = N_GROUPS + g_sel * EXPERTS_PER_GROUP
    in_group = (lane >= first) & (lane < first + EXPERTS_PER_GROUP)
    e_max, _ = first_lane_of_max(logits, in_group)
    e_exp = jnp.where(in_group, jnp.exp(logits - e_max), 0.0)
    prob = e_exp / jnp.sum(e_exp, axis=-1, keepdims=True)
    p1, i1 = first_lane_of_max(prob, in_group)
    p2, i2 = first_lane_of_max(prob, in_group & (lane != i1))
    denom = p1 + p2
    record = jnp.where(lane == 0, (i1 - N_GROUPS).astype(F32),
                       jnp.where(lane == 1, (i2 - N_GROUPS).astype(F32),
                                 jnp.where(lane == 2, g_w * p1 / denom,
                                           jnp.where(lane == 3, g_w * p2 / denom, 0.0))))
    route_ref[...] = record
    route_t_ref[...] = record.T[:ROUTE_ROWS, :]


def _mix_out(x2, oa, ob, sga, sgb, wa, wb, wo, g_ffn, wr, br):
    t = x2.shape[0]
    row = lambda i: (i, 0)
    const = lambda i: (0, 0)
    once = dict(pipeline_mode=pl.Buffered(1))
    return pl.pallas_call(
        _mix_out_kernel,
        grid=(t // MIX_TM,),
        in_specs=[pl.BlockSpec((MIX_TM, D_MODEL), row),
                  pl.BlockSpec((MIX_TM, WIDTH), row), pl.BlockSpec((MIX_TM, WIDTH), row),
                  pl.BlockSpec((MIX_TM, D_MODEL), row), pl.BlockSpec((MIX_TM, D_MODEL), row),
                  pl.BlockSpec((WIDTH, D_MODEL), const, **once), pl.BlockSpec((WIDTH, D_MODEL), const, **once),
                  pl.BlockSpec((D_MODEL, D_MODEL), const, **once), pl.BlockSpec((1, D_MODEL), const),
                  pl.BlockSpec((D_MODEL, 2 * LANES), const, **once),
                  pl.BlockSpec((1, LANES), const)],
        out_specs=[pl.BlockSpec((MIX_TM, D_MODEL), row), pl.BlockSpec((MIX_TM, D_MODEL), row),
                   pl.BlockSpec((MIX_TM, LANES), row), pl.BlockSpec((ROUTE_ROWS, MIX_TM), lambda i: (0, i))],
        out_shape=[jax.ShapeDtypeStruct((t, D_MODEL), F32), jax.ShapeDtypeStruct((t, D_MODEL), F32),
                   jax.ShapeDtypeStruct((t, LANES), F32), jax.ShapeDtypeStruct((ROUTE_ROWS, t), F32)],
        compiler_params=_cparams("parallel"),
        name="mix_out",
    )(x2, oa, ob, sga, sgb, wa, wb, wo, g_ffn, wr, br)


PLAN_TM = 512


def _moe_plan_kernel(route_t_ref, dest_ref, counts_ref, count_ref, start_ref):
    phase = pl.program_id(0)
    step = pl.program_id(1)
    expert = lax.broadcasted_iota(I32, (LANES, PLAN_TM), 0)
    e1 = route_t_ref[0:1, :].astype(I32)
    e2 = route_t_ref[1:2, :].astype(I32)
    hot1 = expert == e1
    hot2 = expert == e2
    hot = jnp.where(hot1 | hot2, 1.0, 0.0)

    @pl.when((phase == 0) & (step == 0))
    def _():
        count_ref[...] = jnp.zeros_like(count_ref)

    @pl.when(phase == 0)
    def _():
        count_ref[...] += jnp.sum(hot, axis=-1, keepdims=True)
        dest_ref[...] = jnp.zeros_like(dest_ref)

    @pl.when((phase == 1) & (step == 0))
    def _():
        counts = jnp.broadcast_to(count_ref[...], (LANES, LANES))
        counts_ref[...] = counts
        padded = jnp.ceil(counts / DISPATCH_BLOCK) * DISPATCH_BLOCK
        sub = lax.broadcasted_iota(I32, (LANES, LANES), 0)
        ends = padded
        shift = 1
        while shift < LANES:
            ends = ends + jnp.where(sub >= shift, pltpu.roll(ends, shift, 0), 0.0)
            shift *= 2
        start_ref[...] = (ends - padded)[:, 0:1]
        count_ref[...] = jnp.zeros_like(count_ref)

    @pl.when(phase == 1)
    def _():
        tok_r = lax.broadcasted_iota(I32, (PLAN_TM, PLAN_TM), 0)
        tok_c = lax.broadcasted_iota(I32, (PLAN_TM, PLAN_TM), 1)
        earlier = jnp.where(tok_r < tok_c, 1.0, 0.0).astype(BF16)
        slot = start_ref[...] + count_ref[...] + _dot(hot.astype(BF16), earlier)
        d1 = jnp.sum(jnp.where(hot1, slot, 0.0), axis=0, keepdims=True)
        d2 = jnp.sum(jnp.where(hot2, slot, 0.0), axis=0, keepdims=True)
        sub = lax.broadcasted_iota(I32, (ROUTE_ROWS, PLAN_TM), 0)
        dest_ref[...] = jnp.where(sub == 0, d1, jnp.where(sub == 1, d2, 0.0)).astype(I32)
        count_ref[...] += jnp.sum(hot, axis=-1, keepdims=True)


def _moe_plan(route_t):
    t = route_t.shape[1]
    return pl.pallas_call(
        _moe_plan_kernel,
        grid=(2, t // PLAN_TM),
        in_specs=[pl.BlockSpec((ROUTE_ROWS, PLAN_TM), lambda ph, i: (0, i))],
        out_specs=[pl.BlockSpec((ROUTE_ROWS, PLAN_TM), lambda ph, i: (0, i * ph)),
                   pl.BlockSpec((LANES, LANES), lambda ph, i: (0, 0))],
        out_shape=[jax.ShapeDtypeStruct((ROUTE_ROWS, t), I32), jax.ShapeDtypeStruct((LANES, LANES), F32)],
        scratch_shapes=[pltpu.VMEM((LANES, 1), F32), pltpu.VMEM((LANES, 1), F32)],
        compiler_params=_cparams("arbitrary", "arbitrary"),
        name="moe_plan",
    )(route_t)


DISP_TM = 1024
DMA_UNROLL = 8


def _moe_dispatch_kernel(tail_ref, d1_ref, d2_ref, xs_ref, out_ref, zeros_ref, zero_sem, sem):
    @pl.when(pl.program_id(0) == 0)
    def _():
        zeros_ref[...] = jnp.zeros_like(zeros_ref)

        def zero_block(e):
            first = pl.multiple_of(tail_ref[e], DISPATCH_BLOCK)
            return pltpu.make_async_copy(zeros_ref, out_ref.at[pl.ds(first, DISPATCH_BLOCK), :], zero_sem)
        for e in range(2 * N_EXPERTS):
            pl.when(tail_ref[e] >= 0)(lambda e=e: zero_block(e).start())
        for e in range(2 * N_EXPERTS):
            pl.when(tail_ref[e] >= 0)(lambda e=e: zero_block(e).wait())

    def row_copy(r, dest):
        return pltpu.make_async_copy(xs_ref.at[pl.ds(r, 1), :], out_ref.at[pl.ds(dest, 1), :], sem)

    def issue(r, carry):
        row_copy(r, d1_ref[r]).start(priority=0)
        row_copy(r, d2_ref[r]).start(priority=1)
        return carry
    lax.fori_loop(0, DISP_TM, issue, 0, unroll=DMA_UNROLL)

    for _ in range(2):
        pltpu.make_async_copy(xs_ref, out_ref.at[pl.ds(0, DISP_TM), :], sem).wait()


def _moe_dispatch(tail_start, d1, d2, xs, n_rows):
    t = xs.shape[0]
    smem = lambda: pl.BlockSpec((DISP_TM,), lambda i: (i,), memory_space=pltpu.SMEM)
    return pl.pallas_call(
        _moe_dispatch_kernel,
        grid=(t // DISP_TM,),
        in_specs=[pl.BlockSpec(memory_space=pltpu.SMEM), smem(), smem(),
                  pl.BlockSpec((DISP_TM, D_MODEL), lambda i: (i, 0))],
        out_specs=pl.BlockSpec(memory_space=pl.ANY),
        out_shape=jax.ShapeDtypeStruct((n_rows, D_MODEL), F32),
        scratch_shapes=[pltpu.VMEM((DISPATCH_BLOCK, D_MODEL), F32), pltpu.SemaphoreType.DMA(()),
                        pltpu.SemaphoreType.DMA(())],
        compiler_params=_cparams("arbitrary"),
        name="moe_dispatch",
    )(tail_start, d1, d2, xs)


def _moe_expert_kernel(block_expert_ref, n_used_ref, xs_ref, w1_ref, w3_ref, w2_ref, ys_ref):
    del block_expert_ref
    live = pl.program_id(0) < n_used_ref[0]

    @pl.when(live)
    def _():
        x = xs_ref[...].astype(BF16)
        h1 = _dot(x, w1_ref[0])
        h3 = _dot(x, w3_ref[0])
        hid = h1 / (1.0 + jnp.exp(-h1)) * h3
        ys_ref[...] = _dot(hid.astype(BF16), w2_ref[0])

    @pl.when(jnp.logical_not(live))
    def _():
        ys_ref[...] = jnp.zeros_like(ys_ref)


def _moe_experts(block_expert, n_used, xs_sorted, w1, w3, w2):
    n_rows = xs_sorted.shape[0]
    grid_spec = pltpu.PrefetchScalarGridSpec(
        num_scalar_prefetch=2,
        grid=(n_rows // DISPATCH_BLOCK,),
        in_specs=[pl.BlockSpec((DISPATCH_BLOCK, D_MODEL), lambda j, be, nu: (jnp.minimum(j, nu[0] - 1), 0)),
                  pl.BlockSpec((1, D_MODEL, EXPERT_FF), lambda j, be, nu: (be[j], 0, 0)),
                  pl.BlockSpec((1, D_MODEL, EXPERT_FF), lambda j, be, nu: (be[j], 0, 0)),
                  pl.BlockSpec((1, EXPERT_FF, D_MODEL), lambda j, be, nu: (be[j], 0, 0))],
        out_specs=pl.BlockSpec((DISPATCH_BLOCK, D_MODEL), lambda j, be, nu: (j, 0)))
    return pl.pallas_call(
        _moe_expert_kernel,
        grid_spec=grid_spec,
        out_shape=jax.ShapeDtypeStruct((n_rows, D_MODEL), F32),
        compiler_params=_cparams("arbitrary"),
        name="moe_experts",
    )(block_expert, n_used, xs_sorted, w1, w3, w2)


COMB_TM = 512


def _moe_combine_kernel(d1_ref, d2_ref, x1_ref, route_ref, g_ref, ys_ref, out_ref, y1_ref, y2_ref, sem):
    def row_copy(src, r, buf):
        return pltpu.make_async_copy(ys_ref.at[pl.ds(src, 1), :], buf.at[pl.ds(r, 1), :], sem)

    def issue(r, carry):
        row_copy(d1_ref[r], r, y1_ref).start(priority=0)
        row_copy(d2_ref[r], r, y2_ref).start(priority=1)
        return carry
    lax.fori_loop(0, COMB_TM, issue, 0, unroll=DMA_UNROLL)

    for buf in (y1_ref, y2_ref):
        pltpu.make_async_copy(ys_ref.at[pl.ds(0, COMB_TM), :], buf, sem).wait()

    route = route_ref[...]
    x2 = x1_ref[...] + (route[:, 2:3] * y1_ref[...] + route[:, 3:4] * y2_ref[...])
    out_ref[...] = x2 * lax.rsqrt(jnp.mean(x2 * x2, axis=-1, keepdims=True) + RMS_EPS) * g_ref[...]


def _moe_combine(d1, d2, x1, route, g_final, ys):
    t = x1.shape[0]
    smem = lambda: pl.BlockSpec((COMB_TM,), lambda i: (i,), memory_space=pltpu.SMEM)
    row = lambda i: (i, 0)
    return pl.pallas_call(
        _moe_combine_kernel,
        grid=(t // COMB_TM,),
        in_specs=[smem(), smem(), pl.BlockSpec((COMB_TM, D_MODEL), row), pl.BlockSpec((COMB_TM, LANES), row),
                  pl.BlockSpec((1, D_MODEL), lambda i: (0, 0)), pl.BlockSpec(memory_space=pl.ANY)],
        out_specs=pl.BlockSpec((COMB_TM, D_MODEL), row),
        out_shape=jax.ShapeDtypeStruct((t, D_MODEL), F32),
        scratch_shapes=[pltpu.VMEM((COMB_TM, D_MODEL), F32), pltpu.VMEM((COMB_TM, D_MODEL), F32),
                        pltpu.SemaphoreType.DMA(())],
        compiler_params=_cparams("arbitrary"),
        name="moe_combine",
    )(d1, d2, x1, route, g_final, ys)


def _hier_moe_tail(x1, xs, route, route_t, w1, w3, w2, g_final):
    t = x1.shape[0]
    dest, counts = _moe_plan(route_t)
    counts = counts[:N_EXPERTS, 0].astype(I32)
    padded = (counts + DISPATCH_BLOCK - 1) // DISPATCH_BLOCK * DISPATCH_BLOCK
    ends = jnp.cumsum(padded)
    n_blocks = (t * 2) // DISPATCH_BLOCK + N_EXPERTS
    block_start = jnp.arange(n_blocks, dtype=I32) * DISPATCH_BLOCK
    block_expert = jnp.minimum(jnp.sum((ends[None, :] <= block_start[:, None]).astype(I32), axis=1), N_EXPERTS - 1)
    n_used = (ends[-1:] // DISPATCH_BLOCK).astype(I32)
    d1, d2 = dest[0], dest[1]
    spare = n_used[0] + jnp.arange(N_EXPERTS, dtype=I32)
    tail_start = jnp.concatenate([jnp.where(padded > 0, ends - DISPATCH_BLOCK, -1),
                                  jnp.where(spare < n_blocks, spare * DISPATCH_BLOCK, -1)]).astype(I32)
    xs_sorted = _moe_dispatch(tail_start, d1, d2, xs, n_blocks * DISPATCH_BLOCK)
    ys = _moe_experts(block_expert, n_used, xs_sorted, w1, w3, w2)
    return _moe_combine(d1, d2, x1, route, g_final, ys)


def kernel(x, g_mix, w_in, w_proj_a, w_proj_b, w_out, g_ffn, w_group, b_group, w_expert, b_expert, w1, w3, w2, g_final):
    batch, seq, d = x.shape
    assert d == D_MODEL and g_mix.shape[0] == 1, "one layer of width D_MODEL"
    x2 = x.reshape(batch * seq, d)
    cos_t, sin_t = _rope_tables(seq)
    qa, ka, va, qb, kb, vb, qi, ki2, wi, sga, sgb, kmean = _in_proj(
        x2, g_mix[0][None, :], _arrange_w_in(w_in[0]), cos_t, sin_t, seq)
    selb = _moba_select(qa, kmean, batch, seq)
    oa = _moba_attn(qa, ka, va, selb, batch, seq)
    ob = _dsa_attend(qb, kb, vb, _dsa_select(qi, ki2, wi, batch, seq), batch, seq)
    wr, br = _router_params(w_group[0], b_group[0], w_expert[0], b_expert[0])
    x1, xs, route, route_t = _mix_out(
        x2, oa, ob, sga, sgb, w_proj_a[0].astype(BF16), w_proj_b[0].astype(BF16), w_out[0].astype(BF16),
        g_ffn[0][None, :], wr, br)
    out = _hier_moe_tail(x1, xs, route, route_t, w1[0].astype(BF16), w3[0].astype(BF16), w2[0].astype(BF16),
                         g_final[None, :])
    return out.reshape(batch, seq, d)


def _router_params(w_group, b_group, w_expert, b_expert):
    w = jnp.pad(jnp.concatenate([w_group, w_expert], axis=1), ((0, 0), (0, LANES - N_GROUPS - N_EXPERTS)))
    b = jnp.pad(jnp.concatenate([b_group, b_expert]), (0, LANES - N_GROUPS - N_EXPERTS))[None, :]
    w_hi = w.astype(BF16)
    w_lo = (w - w_hi.astype(F32)).astype(BF16)
    return jnp.concatenate([w_hi, w_lo], axis=1), b
```

```python
import functools

import jax
import jax.numpy as jnp
from jax import lax
from jax.experimental import pallas as pl
from jax.experimental.pallas import tpu as pltpu

F32 = jnp.float32
BF16 = jnp.bfloat16
I32 = jnp.int32

D_MODEL = 1024
HEAD_DIM = 64
N_HEADS = 8
WIDTH = N_HEADS * HEAD_DIM
N_PAIRS = N_HEADS // 2
MOBA_BLOCK = 256
MOBA_TOPK = 3
DSA_TOPK = 256
IDX_SCALE = float(WIDTH) ** -0.5
ATTN_SCALE = float(HEAD_DIM) ** -0.5
N_GROUPS = 4
EXPERTS_PER_GROUP = 8
N_EXPERTS = N_GROUPS * EXPERTS_PER_GROUP
EXPERT_FF = 512
DISPATCH_BLOCK = 512
ROPE_THETA = 10000.0
RMS_EPS = 1e-6
NEG = -1e30

LANES = 128
VMEM_LIMIT = 56 * 1024 * 1024

C_QA, C_KA, C_VA, C_QB, C_KB, C_VB, C_QI = (i * WIDTH for i in range(7))
C_KI = 7 * WIDTH
C_WI = C_KI + LANES
C_GA = C_WI + WIDTH
C_GB = C_GA + D_MODEL
IN_COLS_PADDED = C_GB + D_MODEL


def _cparams(*semantics):
    return pltpu.CompilerParams(dimension_semantics=semantics, vmem_limit_bytes=VMEM_LIMIT)


def _dot_nt(a, b):
    return lax.dot_general(a, b, (((1,), (1,)), ((), ())), preferred_element_type=F32)


def _dot(a, b):
    return jnp.dot(a, b, preferred_element_type=F32)


IN_TM = 512


def _in_proj_kernel(x_ref, g_ref, w_ref, cos_ref, sin_ref,
                    qa_ref, ka_ref, va_ref, qb_ref, kb_ref, vb_ref, qi_ref, ki_ref, wi_ref,
                    sga_ref, sgb_ref, kmean_ref):
    x = x_ref[...]
    h = x * lax.rsqrt(jnp.mean(x * x, axis=-1, keepdims=True) + RMS_EPS) * g_ref[...]
    hb = h.astype(BF16)
    cos = cos_ref[...]
    sin = sin_ref[...]
    upper_half = (lax.broadcasted_iota(I32, (IN_TM, LANES), 1) & (HEAD_DIM // 2)) != 0

    def rope(v):
        partner = jnp.where(upper_half, pltpu.roll(v, HEAD_DIM // 2, 1), pltpu.roll(v, LANES - HEAD_DIM // 2, 1))
        return v * cos + partner * sin

    def proj(c0, width):
        return _dot(hb, w_ref[:, c0:c0 + width])

    def store_heads(ref, c0, rotary):
        r = proj(c0, WIDTH)
        for j in range(WIDTH // LANES):
            v = r[:, j * LANES:(j + 1) * LANES]
            ref[:, j * LANES:(j + 1) * LANES] = (rope(v) if rotary else v).astype(ref.dtype)

    store_heads(qa_ref, C_QA, True)
    store_heads(va_ref, C_VA, False)
    store_heads(qb_ref, C_QB, True)
    store_heads(kb_ref, C_KB, True)
    store_heads(vb_ref, C_VB, False)
    store_heads(qi_ref, C_QI, True)

    r = proj(C_KA, WIDTH)
    for j in range(WIDTH // LANES):
        v = rope(r[:, j * LANES:(j + 1) * LANES])
        ka_ref[:, j * LANES:(j + 1) * LANES] = v.astype(BF16)
        for blk in range(IN_TM // MOBA_BLOCK):
            kmean_ref[blk, :, j * LANES:(j + 1) * LANES] = jnp.mean(
                v[blk * MOBA_BLOCK:(blk + 1) * MOBA_BLOCK], axis=0, keepdims=True)

    ki_ref[...] = rope(proj(C_KI, LANES)).astype(BF16)
    wi_ref[...] = proj(C_WI, WIDTH) * IDX_SCALE
    sga_ref[...] = (1.0 / (1.0 + jnp.exp(-proj(C_GA, D_MODEL)))).astype(BF16)
    sgb_ref[...] = (1.0 / (1.0 + jnp.exp(-proj(C_GB, D_MODEL)))).astype(BF16)


def _in_proj(x2, g_mix, w_r, cos_t, sin_t, seq):
    t = x2.shape[0]
    n_tiles = t // IN_TM
    tiles_per_seq = seq // IN_TM
    row = lambda i: (i, 0)
    const = lambda i: (0, 0)
    act = lambda w, dt: jax.ShapeDtypeStruct((t, w), dt)
    out_shape = ([act(WIDTH, BF16)] * 7 + [act(LANES, BF16), act(WIDTH, F32), act(D_MODEL, BF16), act(D_MODEL, BF16),
                                           jax.ShapeDtypeStruct((t // MOBA_BLOCK, 1, WIDTH), F32)])
    out_specs = ([pl.BlockSpec((IN_TM, WIDTH), row)] * 7
                 + [pl.BlockSpec((IN_TM, LANES), row), pl.BlockSpec((IN_TM, WIDTH), row),
                    pl.BlockSpec((IN_TM, D_MODEL), row), pl.BlockSpec((IN_TM, D_MODEL), row),
                    pl.BlockSpec((IN_TM // MOBA_BLOCK, 1, WIDTH), lambda i: (i, 0, 0))])
    return pl.pallas_call(
        _in_proj_kernel,
        grid=(n_tiles,),
        in_specs=[pl.BlockSpec((IN_TM, D_MODEL), row),
                  pl.BlockSpec((1, D_MODEL), const),
                  pl.BlockSpec((D_MODEL, IN_COLS_PADDED), const, pipeline_mode=pl.Buffered(1)),
                  pl.BlockSpec((IN_TM, LANES), lambda i: (i % tiles_per_seq, 0)),
                  pl.BlockSpec((IN_TM, LANES), lambda i: (i % tiles_per_seq, 0))],
        out_specs=out_specs,
        out_shape=out_shape,
        compiler_params=_cparams("parallel"),
        name="in_proj",
    )(x2, g_mix, w_r, cos_t, sin_t)


def _moba_select_kernel(qa_ref, km_ref, selb_ref):
    n_blocks = km_ref.shape[1]
    own = lax.broadcasted_iota(I32, (qa_ref.shape[0], LANES), 0) // MOBA_BLOCK
    km = km_ref[0]
    col_head = lax.broadcasted_iota(I32, (N_HEADS, WIDTH), 1) // HEAD_DIM
    head_mask = col_head == lax.broadcasted_iota(I32, (N_HEADS, WIDTH), 0)
    rows = [jnp.where(head_mask, km[n:n + 1, :], 0.0) for n in range(n_blocks)]
    rows.append(jnp.zeros((LANES - n_blocks * N_HEADS, WIDTH), F32))
    km_t = jnp.concatenate(rows, axis=0).astype(BF16)
    gate = _dot_nt(qa_ref[...], km_t)
    lane = lax.broadcasted_iota(I32, gate.shape, 1)
    past = (lane // N_HEADS) < own
    g = jnp.where(past, gate, NEG)
    rank = jnp.zeros(gate.shape, I32)
    for r in range(1, n_blocks):
        later = pltpu.roll(g, LANES - N_HEADS * r, 1)
        earlier = pltpu.roll(g, N_HEADS * r, 1)
        rank = rank + (later > g).astype(I32) + (earlier >= g).astype(I32)
    chosen = jnp.where(past & (rank < MOBA_TOPK), 1.0, 0.0).astype(BF16)
    src = lax.broadcasted_iota(I32, (LANES, N_PAIRS * LANES), 0)
    dst = lax.broadcasted_iota(I32, (LANES, N_PAIRS * LANES), 1)
    src_n, src_h = src // N_HEADS, src % N_HEADS
    dst_pair, dst_w = dst // LANES, dst % LANES
    expand = ((src_n < n_blocks) & (dst_w < 2 * N_HEADS) & (dst_w % N_HEADS == src_n)
              & (dst_pair * 2 + dst_w // N_HEADS == src_h))
    hit = _dot(chosen, jnp.where(expand, 1.0, 0.0).astype(BF16))
    selb_ref[...] = jnp.where(hit > 0.5, 0.0, NEG).astype(selb_ref.dtype)


def _moba_select(qa, kmean, batch, seq):
    n_blocks = seq // MOBA_BLOCK
    assert n_blocks * N_HEADS <= LANES and n_blocks <= N_HEADS
    t = batch * seq
    return pl.pallas_call(
        _moba_select_kernel,
        grid=(batch,),
        in_specs=[pl.BlockSpec((seq, WIDTH), lambda b: (b, 0)),
                  pl.BlockSpec((1, n_blocks, WIDTH), lambda b: (b, 0, 0))],
        out_specs=pl.BlockSpec((seq, N_PAIRS * LANES), lambda b: (b, 0)),
        out_shape=jax.ShapeDtypeStruct((t, N_PAIRS * LANES), BF16),
        compiler_params=_cparams("parallel"),
        name="moba_select",
    )(qa, kmean.reshape(batch, n_blocks, WIDTH))


def _stack_heads(q2):
    lane = lax.broadcasted_iota(I32, q2.shape, 1)
    zero = jnp.zeros_like(q2)
    return jnp.concatenate([jnp.where(lane < HEAD_DIM, q2, zero), jnp.where(lane >= HEAD_DIM, q2, zero)], axis=0)


def _unstack_heads(o):
    rows = o.shape[0] // 2
    lane = lax.broadcasted_iota(I32, (rows, LANES), 1)
    return jnp.where(lane < HEAD_DIM, o[:rows], o[rows:])


def _attend_pairs(pairs, scores, v_ref, n_keys, o_ref):
    tops = [jnp.max(s, axis=-1, keepdims=True) for s in scores]
    probs = [jnp.exp(s - m) for s, m in zip(scores, tops)]
    outs = [_dot(p.astype(BF16), v_ref[0:n_keys, pair]) for pair, p in zip(pairs, probs)]
    sums = [jnp.sum(p, axis=-1, keepdims=True) for p in probs]
    for pair, o, l in zip(pairs, outs, sums):
        o_ref[:, pair] = _unstack_heads(o / l).astype(o_ref.dtype)


PAIRS_PER_STEP = 2
STEP_LANES = PAIRS_PER_STEP * LANES


def _moba_attn_kernel(q_ref, k_ref, v_ref, selb_ref, o_ref):
    own = pl.program_id(2)
    n_blocks = k_ref.shape[0] // MOBA_BLOCK
    q_pos = lax.broadcasted_iota(I32, (2 * MOBA_BLOCK, MOBA_BLOCK), 0) % MOBA_BLOCK
    causal = lax.broadcasted_iota(I32, (2 * MOBA_BLOCK, MOBA_BLOCK), 1) <= q_pos
    for i in range(n_blocks):
        @pl.when(own == i)
        def _(i=i):
            n_keys = (i + 1) * MOBA_BLOCK
            key_blk = lax.broadcasted_iota(I32, (n_keys, LANES), 0) // MOBA_BLOCK
            key_lane = lax.broadcasted_iota(I32, (n_keys, LANES), 1)
            indicator = jnp.where((key_lane < 2 * N_HEADS) & (key_lane % N_HEADS == key_blk) & (key_blk < i),
                                  1.0, 0.0).astype(BF16)
            q_lane = lax.broadcasted_iota(I32, (MOBA_BLOCK, LANES), 1)
            pairs = [slice(u * LANES, (u + 1) * LANES) for u in range(PAIRS_PER_STEP)]
            scores = []
            for pair in pairs:
                selb = selb_ref[:, pair]
                zero = jnp.zeros_like(selb)
                bias_rows = jnp.concatenate([jnp.where(q_lane < N_HEADS, selb, zero),
                                             jnp.where((q_lane >= N_HEADS) & (q_lane < 2 * N_HEADS), selb, zero)], axis=0)
                s = _dot_nt(jnp.concatenate([_stack_heads(q_ref[:, pair]), bias_rows], axis=1),
                            jnp.concatenate([k_ref[0:n_keys, pair], indicator], axis=1))
                own_s = jnp.where(causal, s[:, i * MOBA_BLOCK:], NEG)
                scores.append(own_s if i == 0 else jnp.concatenate([s[:, :i * MOBA_BLOCK], own_s], axis=1))
            _attend_pairs(pairs, scores, v_ref, n_keys, o_ref)


def _moba_attn(qa, ka, va, selb, batch, seq):
    n_blocks = seq // MOBA_BLOCK
    t = batch * seq
    q_map = lambda b, p, i: (b * n_blocks + i, p)
    kv_map = lambda b, p, i: (b, p)
    return pl.pallas_call(
        _moba_attn_kernel,
        grid=(batch, N_PAIRS // PAIRS_PER_STEP, n_blocks),
        in_specs=[pl.BlockSpec((MOBA_BLOCK, STEP_LANES), q_map),
                  pl.BlockSpec((seq, STEP_LANES), kv_map),
                  pl.BlockSpec((seq, STEP_LANES), kv_map),
                  pl.BlockSpec((MOBA_BLOCK, STEP_LANES), q_map)],
        out_specs=pl.BlockSpec((MOBA_BLOCK, STEP_LANES), q_map),
        out_shape=jax.ShapeDtypeStruct((t, WIDTH), BF16),
        compiler_params=_cparams("parallel", "parallel", "arbitrary"),
        name="moba_attn",
    )(qa, ka, va, selb)


def _rope_tables(seq):
    inv = jnp.power(ROPE_THETA, -jnp.arange(0, HEAD_DIM, 2, dtype=F32) / HEAD_DIM)
    ang = jnp.arange(seq, dtype=F32)[:, None] * inv[None, :]
    cos, sin = jnp.cos(ang), jnp.sin(ang)
    reps = LANES // HEAD_DIM
    return jnp.tile(jnp.concatenate([cos, cos], axis=-1), (1, reps)), jnp.tile(jnp.concatenate([-sin, sin], axis=-1), (1, reps))


def _arrange_w_in(w_in):
    sizes = (WIDTH,) * 7 + (HEAD_DIM, N_HEADS, D_MODEL, D_MODEL)
    offs = [0]
    for s in sizes:
        offs.append(offs[-1] + s)
    seg = [w_in[:, offs[i]:offs[i + 1]] for i in range(len(sizes))]
    qa, ka, va, qb, kb, vb, qi, ki, wi, ga, gb = seg
    wi_pairs = jnp.pad(wi.reshape(-1, N_PAIRS, 2), ((0, 0), (0, 0), (0, LANES - 2))).reshape(-1, WIDTH)
    w = jnp.concatenate([qa * ATTN_SCALE, ka, va, qb * ATTN_SCALE, kb, vb, qi, ki, ki, wi_pairs, ga, gb], axis=1)
    return w.astype(BF16)


DSA_TQ = 256
DSA_ROWS = 64
INT_MIN = -(2 ** 31)


def _count_lanes(hit):
    acc = hit[:, :LANES]
    for j in range(1, hit.shape[1] // LANES):
        acc = acc + hit[:, j * LANES:(j + 1) * LANES]
    return jnp.sum(acc, axis=-1, keepdims=True)


def _dsa_score_pair(jobs, ki_ref):
    logits = [_dot_nt(_stack_heads(qi_ref[...]), ki_ref[0:(i + 1) * DSA_TQ, :]) for i, qi_ref, _, _ in jobs]
    parts = []
    for (_, _, wi_ref, _), logit in zip(jobs, logits):
        wi = wi_ref[...]
        parts.append(wi[:, 0:1] * jnp.maximum(logit[:DSA_TQ], 0.0) + wi[:, 1:2] * jnp.maximum(logit[DSA_TQ:], 0.0))
    pair = pl.program_id(2)

    @pl.when(pair == 0)
    def _():
        for (i, _, _, score_ref), part in zip(jobs, parts):
            score_ref[:, 0:(i + 1) * DSA_TQ] = part

    @pl.when(pair > 0)
    def _():
        for (i, _, _, score_ref), part in zip(jobs, parts):
            score_ref[:, 0:(i + 1) * DSA_TQ] += part


def _dsa_thresholds(jobs):
    def as_float(code):
        return pltpu.bitcast(code ^ ((code >> 31) & 0x7FFFFFFF), F32)

    n_rg = DSA_TQ // DSA_ROWS
    units = [(i, score_ref, rg) for i, score_ref in jobs for rg in range(n_rg)]

    def step(unit, t_old, cand, cand_float):
        i, score_ref, rg = unit
        k = score_ref[rg * DSA_ROWS:(rg + 1) * DSA_ROWS, 0:(i + 1) * DSA_TQ]
        n = _count_lanes(jnp.where(k >= cand_float, 1.0, 0.0))
        return jnp.where(n >= DSA_TOPK, cand, t_old)

    ts = tuple(step(u, jnp.full((DSA_ROWS, 1), INT_MIN, I32), jnp.zeros((DSA_ROWS, 1), I32),
                    jnp.zeros((DSA_ROWS, 1), F32)) for u in units)
    flips = tuple(jnp.where(t >= 0, 0, 0x7FFFFFFF) for t in ts)

    def bit_body(ib, ts):
        bit = jnp.left_shift(jnp.int32(1), 30 - ib)
        return tuple(step(u, t, t | bit, pltpu.bitcast((t | bit) ^ f, F32)) for u, t, f in zip(units, ts, flips))

    ts = lax.fori_loop(0, 31, bit_body, ts)
    out = []
    for j in range(len(jobs)):
        code = jnp.concatenate(ts[j * n_rg:(j + 1) * n_rg], axis=0)
        out.append(jnp.where(code == INT_MIN, -jnp.inf, as_float(code)))
    return out


def _dsa_write_bias(i, score_ref, thr, bias_ref, first_row):
    n_keys = (i + 1) * DSA_TQ
    rows = slice(first_row, first_row + DSA_TQ)
    row = lax.broadcasted_iota(I32, (DSA_TQ, DSA_TQ), 0)
    col = lax.broadcasted_iota(I32, (DSA_TQ, DSA_TQ), 1)
    keys = score_ref[:, 0:n_keys]
    need = float(DSA_TOPK) - _count_lanes(jnp.where(keys > thr, 1.0, 0.0))
    strictly_before = jnp.where(row < col, 1.0, 0.0).astype(BF16)
    ties_seen = jnp.zeros((DSA_TQ, 1), F32)
    for t in range(i + 1):
        k = keys[:, t * DSA_TQ:(t + 1) * DSA_TQ]
        tie = jnp.where(k == thr, 1.0, 0.0)
        ties_before = ties_seen + _dot(tie.astype(BF16), strictly_before)
        chosen = (k > thr) | ((k == thr) & (ties_before < need))
        if t == i:
            chosen = chosen & (col <= row)
        bias_ref[rows, t * DSA_TQ:(t + 1) * DSA_TQ] = jnp.where(chosen, 0.0, NEG).astype(bias_ref.dtype)
        ties_seen = ties_seen + jnp.sum(tie, axis=-1, keepdims=True)
    if n_keys < bias_ref.shape[1]:
        bias_ref[rows, n_keys:] = jnp.full((DSA_TQ, bias_ref.shape[1] - n_keys), NEG, bias_ref.dtype)


def _dsa_select_kernel(qi_lo_ref, qi_hi_ref, ki_ref, wi_lo_ref, wi_hi_ref, bias_ref, score_lo_ref, score_hi_ref):
    n_chunks = ki_ref.shape[0] // DSA_TQ
    for j in range(n_chunks // 2):
        @pl.when(pl.program_id(1) == j)
        def _(j=j):
            lo, hi = j, n_chunks - 1 - j
            _dsa_score_pair([(lo, qi_lo_ref, wi_lo_ref, score_lo_ref), (hi, qi_hi_ref, wi_hi_ref, score_hi_ref)], ki_ref)

            @pl.when(pl.program_id(2) == N_PAIRS - 1)
            def _():
                row = lax.broadcasted_iota(I32, (DSA_TQ, DSA_TQ), 0)
                col = lax.broadcasted_iota(I32, (DSA_TQ, DSA_TQ), 1)
                jobs = [(lo, score_lo_ref), (hi, score_hi_ref)]
                for i, score_ref in jobs:
                    diag = slice(i * DSA_TQ, (i + 1) * DSA_TQ)
                    score_ref[:, diag] = jnp.where(col <= row, score_ref[:, diag], NEG)
                for k, ((i, score_ref), thr) in enumerate(zip(jobs, _dsa_thresholds(jobs))):
                    _dsa_write_bias(i, score_ref, thr, bias_ref, k * DSA_TQ)


def _dsa_bias_block(c, n_chunks):
    return jnp.where(c < n_chunks // 2, 2 * c, 2 * (n_chunks - 1 - c) + 1)


def _dsa_select(qi, ki2, wi, batch, seq):
    assert seq % (2 * DSA_TQ) == 0 and min(DSA_TOPK, seq // 4) == DSA_TOPK
    n_chunks = seq // DSA_TQ
    lo_map = lambda b, j, p: (b * n_chunks + j, p)
    hi_map = lambda b, j, p: (b * n_chunks + n_chunks - 1 - j, p)
    return pl.pallas_call(
        _dsa_select_kernel,
        grid=(batch, n_chunks // 2, N_PAIRS),
        in_specs=[pl.BlockSpec((DSA_TQ, LANES), lo_map), pl.BlockSpec((DSA_TQ, LANES), hi_map),
                  pl.BlockSpec((seq, LANES), lambda b, j, p: (b, 0)),
                  pl.BlockSpec((DSA_TQ, LANES), lo_map), pl.BlockSpec((DSA_TQ, LANES), hi_map)],
        out_specs=pl.BlockSpec((2 * DSA_TQ, seq), lambda b, j, p: (b * (n_chunks // 2) + j, 0)),
        out_shape=jax.ShapeDtypeStruct((batch * seq, seq), BF16),
        scratch_shapes=[pltpu.VMEM((DSA_TQ, seq), F32), pltpu.VMEM((DSA_TQ, seq), F32)],
        compiler_params=_cparams("parallel", "arbitrary", "arbitrary"),
        name="dsa_select",
    )(qi, qi, ki2, wi, wi)


def _dsa_attend_kernel(q_ref, k_ref, v_ref, bias_ref, o_ref):
    for i in range(k_ref.shape[0] // DSA_TQ):
        @pl.when(pl.program_id(1) == i)
        def _(i=i):
            n_keys = (i + 1) * DSA_TQ
            bias = bias_ref[:, 0:n_keys].astype(F32)
            bias = jnp.concatenate([bias, bias], axis=0)
            pairs = [slice(u * LANES, (u + 1) * LANES) for u in range(PAIRS_PER_STEP)]
            scores = [_dot_nt(_stack_heads(q_ref[:, pair]), k_ref[0:n_keys, pair]) + bias for pair in pairs]
            _attend_pairs(pairs, scores, v_ref, n_keys, o_ref)


def _dsa_attend(qb, kb, vb, bias, batch, seq):
    n_chunks = seq // DSA_TQ
    q_map = lambda b, c, p: (b * n_chunks + c, p)
    kv_map = lambda b, c, p: (b, p)
    return pl.pallas_call(
        _dsa_attend_kernel,
        grid=(batch, n_chunks, N_PAIRS // PAIRS_PER_STEP),
        in_specs=[pl.BlockSpec((DSA_TQ, STEP_LANES), q_map),
                  pl.BlockSpec((seq, STEP_LANES), kv_map),
                  pl.BlockSpec((seq, STEP_LANES), kv_map),
                  pl.BlockSpec((DSA_TQ, seq), lambda b, c, p: (b * n_chunks + _dsa_bias_block(c, n_chunks), 0))],
        out_specs=pl.BlockSpec((DSA_TQ, STEP_LANES), q_map),
        out_shape=jax.ShapeDtypeStruct((batch * seq, WIDTH), BF16),
        compiler_params=_cparams("parallel", "parallel", "arbitrary"),
        name="dsa_attend",
    )(qb, kb, vb, bias)


MIX_TM = 512
ROUTE_ROWS = 8


def _mix_out_kernel(x_ref, oa_ref, ob_ref, sga_ref, sgb_ref, wa_ref, wb_ref, wo_ref, g_ref, wr_ref, br_ref,
                    x1_ref, xs_ref, route_ref, route_t_ref):
    mixed = (sga_ref[...].astype(F32) * _dot(oa_ref[...], wa_ref[...])
             + sgb_ref[...].astype(F32) * _dot(ob_ref[...], wb_ref[...]))
    x1 = x_ref[...] + _dot(mixed.astype(BF16), wo_ref[...])
    x1_ref[...] = x1
    hn = x1 * lax.rsqrt(jnp.mean(x1 * x1, axis=-1, keepdims=True) + RMS_EPS) * g_ref[...]
    xs_ref[...] = hn

    hi = hn.astype(BF16)
    lo = (hn - hi.astype(F32)).astype(BF16)
    both = _dot(hi, wr_ref[...])
    logits = both[:, :LANES] + both[:, LANES:] + _dot(lo, wr_ref[:, :LANES]) + br_ref[...]
    lane = lax.broadcasted_iota(I32, logits.shape, 1)
    far = jnp.int32(LANES)

    def first_lane_of_max(v, valid):
        top = jnp.max(jnp.where(valid, v, NEG), axis=-1, keepdims=True)
        return top, jnp.min(jnp.where(valid & (v == top), lane, far), axis=-1, keepdims=True)

    is_group = lane < N_GROUPS
    g_max, g_sel = first_lane_of_max(logits, is_group)
    g_w = 1.0 / jnp.sum(jnp.where(is_group, jnp.exp(logits - g_max), 0.0), axis=-1, keepdims=True)
    first = N_GROUPS + g_sel * EXPERTS_PER_GROUP
    in_group = (lane >= first) & (lane < first + EXPERTS_PER_GROUP)
    e_max, _ = first_lane_of_max(logits, in_group)
    e_exp = jnp.where(in_group, jnp.exp(logits - e_max), 0.0)
    prob = e_exp / jnp.sum(e_exp, axis=-1, keepdims=True)
    p1, i1 = first_lane_of_max(prob, in_group)
    p2, i2 = first_lane_of_max(prob, in_group & (lane != i1))
    denom = p1 + p2
    record = jnp.where(lane == 0, (i1 - N_GROUPS).astype(F32),
                       jnp.where(lane == 1, (i2 - N_GROUPS).astype(F32),
                                 jnp.where(lane == 2, g_w * p1 / denom,
                                           jnp.where(lane == 3, g_w * p2 / denom, 0.0))))
    route_ref[...] = record
    route_t_ref[...] = record.T[:ROUTE_ROWS, :]


def _mix_out(x2, oa, ob, sga, sgb, wa, wb, wo, g_ffn, wr, br):
    t = x2.shape[0]
    row = lambda i: (i, 0)
    const = lambda i: (0, 0)
    once = dict(pipeline_mode=pl.Buffered(1))
    return pl.pallas_call(
        _mix_out_kernel,
        grid=(t // MIX_TM,),
        in_specs=[pl.BlockSpec((MIX_TM, D_MODEL), row),
                  pl.BlockSpec((MIX_TM, WIDTH), row), pl.BlockSpec((MIX_TM, WIDTH), row),
                  pl.BlockSpec((MIX_TM, D_MODEL), row), pl.BlockSpec((MIX_TM, D_MODEL), row),
                  pl.BlockSpec((WIDTH, D_MODEL), const, **once), pl.BlockSpec((WIDTH, D_MODEL), const, **once),
                  pl.BlockSpec((D_MODEL, D_MODEL), const, **once), pl.BlockSpec((1, D_MODEL), const),
                  pl.BlockSpec((D_MODEL, 2 * LANES), const, **once),
                  pl.BlockSpec((1, LANES), const)],
        out_specs=[pl.BlockSpec((MIX_TM, D_MODEL), row), pl.BlockSpec((MIX_TM, D_MODEL), row),
                   pl.BlockSpec((MIX_TM, LANES), row), pl.BlockSpec((ROUTE_ROWS, MIX_TM), lambda i: (0, i))],
        out_shape=[jax.ShapeDtypeStruct((t, D_MODEL), F32), jax.ShapeDtypeStruct((t, D_MODEL), F32),
                   jax.ShapeDtypeStruct((t, LANES), F32), jax.ShapeDtypeStruct((ROUTE_ROWS, t), F32)],
        compiler_params=_cparams("parallel"),
        name="mix_out",
    )(x2, oa, ob, sga, sgb, wa, wb, wo, g_ffn, wr, br)


PLAN_TM = 512


def _moe_plan_kernel(route_t_ref, dest_ref, counts_ref, count_ref, start_ref):
    phase = pl.program_id(0)
    step = pl.program_id(1)
    expert = lax.broadcasted_iota(I32, (LANES, PLAN_TM), 0)
    e1 = route_t_ref[0:1, :].astype(I32)
    e2 = route_t_ref[1:2, :].astype(I32)
    hot1 = expert == e1
    hot2 = expert == e2
    hot = jnp.where(hot1 | hot2, 1.0, 0.0)

    @pl.when((phase == 0) & (step == 0))
    def _():
        count_ref[...] = jnp.zeros_like(count_ref)

    @pl.when(phase == 0)
    def _():
        count_ref[...] += jnp.sum(hot, axis=-1, keepdims=True)
        dest_ref[...] = jnp.zeros_like(dest_ref)

    @pl.when((phase == 1) & (step == 0))
    def _():
        counts = jnp.broadcast_to(count_ref[...], (LANES, LANES))
        counts_ref[...] = counts
        padded = jnp.ceil(counts / DISPATCH_BLOCK) * DISPATCH_BLOCK
        sub = lax.broadcasted_iota(I32, (LANES, LANES), 0)
        ends = padded
        shift = 1
        while shift < LANES:
            ends = ends + jnp.where(sub >= shift, pltpu.roll(ends, shift, 0), 0.0)
            shift *= 2
        start_ref[...] = (ends - padded)[:, 0:1]
        count_ref[...] = jnp.zeros_like(count_ref)

    @pl.when(phase == 1)
    def _():
        tok_r = lax.broadcasted_iota(I32, (PLAN_TM, PLAN_TM), 0)
        tok_c = lax.broadcasted_iota(I32, (PLAN_TM, PLAN_TM), 1)
        earlier = jnp.where(tok_r < tok_c, 1.0, 0.0).astype(BF16)
        slot = start_ref[...] + count_ref[...] + _dot(hot.astype(BF16), earlier)
        d1 = jnp.sum(jnp.where(hot1, slot, 0.0), axis=0, keepdims=True)
        d2 = jnp.sum(jnp.where(hot2, slot, 0.0), axis=0, keepdims=True)
        sub = lax.broadcasted_iota(I32, (ROUTE_ROWS, PLAN_TM), 0)
        dest_ref[...] = jnp.where(sub == 0, d1, jnp.where(sub == 1, d2, 0.0)).astype(I32)
        count_ref[...] += jnp.sum(hot, axis=-1, keepdims=True)


def _moe_plan(route_t):
    t = route_t.shape[1]
    return pl.pallas_call(
        _moe_plan_kernel,
        grid=(2, t // PLAN_TM),
        in_specs=[pl.BlockSpec((ROUTE_ROWS, PLAN_TM), lambda ph, i: (0, i))],
        out_specs=[pl.BlockSpec((ROUTE_ROWS, PLAN_TM), lambda ph, i: (0, i * ph)),
                   pl.BlockSpec((LANES, LANES), lambda ph, i: (0, 0))],
        out_shape=[jax.ShapeDtypeStruct((ROUTE_ROWS, t), I32), jax.ShapeDtypeStruct((LANES, LANES), F32)],
        scratch_shapes=[pltpu.VMEM((LANES, 1), F32), pltpu.VMEM((LANES, 1), F32)],
        compiler_params=_cparams("arbitrary", "arbitrary"),
        name="moe_plan",
    )(route_t)


DISP_TM = 1024
DMA_UNROLL = 8


def _moe_dispatch_kernel(tail_ref, d1_ref, d2_ref, xs_ref, out_ref, zeros_ref, zero_sem, sem):
    @pl.when(pl.program_id(0) == 0)
    def _():
        zeros_ref[...] = jnp.zeros_like(zeros_ref)

        def zero_block(e):
            first = pl.multiple_of(tail_ref[e], DISPATCH_BLOCK)
            return pltpu.make_async_copy(zeros_ref, out_ref.at[pl.ds(first, DISPATCH_BLOCK), :], zero_sem)
        for e in range(2 * N_EXPERTS):
            pl.when(tail_ref[e] >= 0)(lambda e=e: zero_block(e).start())
        for e in range(2 * N_EXPERTS):
            pl.when(tail_ref[e] >= 0)(lambda e=e: zero_block(e).wait())

    def row_copy(r, dest):
        return pltpu.make_async_copy(xs_ref.at[pl.ds(r, 1), :], out_ref.at[pl.ds(dest, 1), :], sem)

    def issue(r, carry):
        row_copy(r, d1_ref[r]).start(priority=0)
        row_copy(r, d2_ref[r]).start(priority=1)
        return carry
    lax.fori_loop(0, DISP_TM, issue, 0, unroll=DMA_UNROLL)

    for _ in range(2):
        pltpu.make_async_copy(xs_ref, out_ref.at[pl.ds(0, DISP_TM), :], sem).wait()


def _moe_dispatch(tail_start, d1, d2, xs, n_rows):
    t = xs.shape[0]
    smem = lambda: pl.BlockSpec((DISP_TM,), lambda i: (i,), memory_space=pltpu.SMEM)
    return pl.pallas_call(
        _moe_dispatch_kernel,
        grid=(t // DISP_TM,),
        in_specs=[pl.BlockSpec(memory_space=pltpu.SMEM), smem(), smem(),
                  pl.BlockSpec((DISP_TM, D_MODEL), lambda i: (i, 0))],
        out_specs=pl.BlockSpec(memory_space=pl.ANY),
        out_shape=jax.ShapeDtypeStruct((n_rows, D_MODEL), F32),
        scratch_shapes=[pltpu.VMEM((DISPATCH_BLOCK, D_MODEL), F32), pltpu.SemaphoreType.DMA(()),
                        pltpu.SemaphoreType.DMA(())],
        compiler_params=_cparams("arbitrary"),
        name="moe_dispatch",
    )(tail_start, d1, d2, xs)


def _moe_expert_kernel(block_expert_ref, n_used_ref, xs_ref, w1_ref, w3_ref, w2_ref, ys_ref):
    del block_expert_ref
    live = pl.program_id(0) < n_used_ref[0]

    @pl.when(live)
    def _():
        x = xs_ref[...]
        h1 = _dot(x, w1_ref[0])
        h3 = _dot(x, w3_ref[0])
        hid = h1 / (1.0 + jnp.exp(-h1)) * h3
        ys_ref[...] = _dot(hid, w2_ref[0])

    @pl.when(jnp.logical_not(live))
    def _():
        ys_ref[...] = jnp.zeros_like(ys_ref)


def _moe_experts(block_expert, n_used, xs_sorted, w1, w3, w2):
    n_rows = xs_sorted.shape[0]
    grid_spec = pltpu.PrefetchScalarGridSpec(
        num_scalar_prefetch=2,
        grid=(n_rows // DISPATCH_BLOCK,),
        in_specs=[pl.BlockSpec((DISPATCH_BLOCK, D_MODEL), lambda j, be, nu: (jnp.minimum(j, nu[0] - 1), 0)),
                  pl.BlockSpec((1, D_MODEL, EXPERT_FF), lambda j, be, nu: (be[j], 0, 0)),
                  pl.BlockSpec((1, D_MODEL, EXPERT_FF), lambda j, be, nu: (be[j], 0, 0)),
                  pl.BlockSpec((1, EXPERT_FF, D_MODEL), lambda j, be, nu: (be[j], 0, 0))],
        out_specs=pl.BlockSpec((DISPATCH_BLOCK, D_MODEL), lambda j, be, nu: (j, 0)))
    return pl.pallas_call(
        _moe_expert_kernel,
        grid_spec=grid_spec,
        out_shape=jax.ShapeDtypeStruct((n_rows, D_MODEL), F32),
        compiler_params=_cparams("arbitrary"),
        name="moe_experts",
    )(block_expert, n_used, xs_sorted, w1, w3, w2)


COMB_TM = 512


def _moe_combine_kernel(d1_ref, d2_ref, x1_ref, route_ref, g_ref, ys_ref, out_ref, y1_ref, y2_ref, sem):
    def row_copy(src, r, buf):
        return pltpu.make_async_copy(ys_ref.at[pl.ds(src, 1), :], buf.at[pl.ds(r, 1), :], sem)

    def issue(r, carry):
        row_copy(d1_ref[r], r, y1_ref).start(priority=0)
        row_copy(d2_ref[r], r, y2_ref).start(priority=1)
        return carry
    lax.fori_loop(0, COMB_TM, issue, 0, unroll=DMA_UNROLL)

    for buf in (y1_ref, y2_ref):
        pltpu.make_async_copy(ys_ref.at[pl.ds(0, COMB_TM), :], buf, sem).wait()

    route = route_ref[...]
    x2 = x1_ref[...] + (route[:, 2:3] * y1_ref[...] + route[:, 3:4] * y2_ref[...])
    out_ref[...] = x2 * lax.rsqrt(jnp.mean(x2 * x2, axis=-1, keepdims=True) + RMS_EPS) * g_ref[...]


def _moe_combine(d1, d2, x1, route, g_final, ys):
    t = x1.shape[0]
    smem = lambda: pl.BlockSpec((COMB_TM,), lambda i: (i,), memory_space=pltpu.SMEM)
    row = lambda i: (i, 0)
    return pl.pallas_call(
        _moe_combine_kernel,
        grid=(t // COMB_TM,),
        in_specs=[smem(), smem(), pl.BlockSpec((COMB_TM, D_MODEL), row), pl.BlockSpec((COMB_TM, LANES), row),
                  pl.BlockSpec((1, D_MODEL), lambda i: (0, 0)), pl.BlockSpec(memory_space=pl.ANY)],
        out_specs=pl.BlockSpec((COMB_TM, D_MODEL), row),
        out_shape=jax.ShapeDtypeStruct((t, D_MODEL), F32),
        scratch_shapes=[pltpu.VMEM((COMB_TM, D_MODEL), F32), pltpu.VMEM((COMB_TM, D_MODEL), F32),
                        pltpu.SemaphoreType.DMA(())],
        compiler_params=_cparams("arbitrary"),
        name="moe_combine",
    )(d1, d2, x1, route, g_final, ys)


def _hier_moe_tail(x1, xs, route, route_t, w1, w3, w2, g_final):
    t = x1.shape[0]
    dest, counts = _moe_plan(route_t)
    counts = counts[:N_EXPERTS, 0].astype(I32)
    padded = (counts + DISPATCH_BLOCK - 1) // DISPATCH_BLOCK * DISPATCH_BLOCK
    ends = jnp.cumsum(padded)
    n_blocks = (t * 2) // DISPATCH_BLOCK + N_EXPERTS
    block_start = jnp.arange(n_blocks, dtype=I32) * DISPATCH_BLOCK
    block_expert = jnp.minimum(jnp.sum((ends[None, :] <= block_start[:, None]).astype(I32), axis=1), N_EXPERTS - 1)
    n_used = (ends[-1:] // DISPATCH_BLOCK).astype(I32)
    d1, d2 = dest[0], dest[1]
    spare = n_used[0] + jnp.arange(N_EXPERTS, dtype=I32)
    tail_start = jnp.concatenate([jnp.where(padded > 0, ends - DISPATCH_BLOCK, -1),
                                  jnp.where(spare < n_blocks, spare * DISPATCH_BLOCK, -1)]).astype(I32)
    xs_sorted = _moe_dispatch(tail_start, d1, d2, xs, n_blocks * DISPATCH_BLOCK)
    ys = _moe_experts(block_expert, n_used, xs_sorted, w1, w3, w2)
    return _moe_combine(d1, d2, x1, route, g_final, ys)


def kernel(x, g_mix, w_in, w_proj_a, w_proj_b, w_out, g_ffn, w_group, b_group, w_expert, b_expert, w1, w3, w2, g_final):
    batch, seq, d = x.shape
    assert d == D_MODEL and g_mix.shape[0] == 1, "one layer of width D_MODEL"
    x2 = x.reshape(batch * seq, d)
    cos_t, sin_t = _rope_tables(seq)
    qa, ka, va, qb, kb, vb, qi, ki2, wi, sga, sgb, kmean = _in_proj(
        x2, g_mix[0][None, :], _arrange_w_in(w_in[0]), cos_t, sin_t, seq)
    selb = _moba_select(qa, kmean, batch, seq)
    oa = _moba_attn(qa, ka, va, selb, batch, seq)
    ob = _dsa_attend(qb, kb, vb, _dsa_select(qi, ki2, wi, batch, seq), batch, seq)
    wr, br = _router_params(w_group[0], b_group[0], w_expert[0], b_expert[0])
    x1, xs, route, route_t = _mix_out(
        x2, oa, ob, sga, sgb, w_proj_a[0].astype(BF16), w_proj_b[0].astype(BF16), w_out[0].astype(BF16),
        g_ffn[0][None, :], wr, br)
    out = _hier_moe_tail(x1, xs, route, route_t, w1[0], w3[0], w2[0],
                         g_final[None, :])
    return out.reshape(batch, seq, d)


def _router_params(w_group, b_group, w_expert, b_expert):
    w = jnp.pad(jnp.concatenate([w_group, w_expert], axis=1), ((0, 0), (0, LANES - N_GROUPS - N_EXPERTS)))
    b = jnp.pad(jnp.concatenate([b_group, b_expert]), (0, LANES - N_GROUPS - N_EXPERTS))[None, :]
    w_hi = w.astype(BF16)
    w_lo = (w - w_hi.astype(F32)).astype(BF16)
    return jnp.concatenate([w_hi, w_lo], axis=1), b
```
